```python
import math
import jax, jax.numpy as jnp
from jax import lax
import numpy as np

D_MODEL = 1024
BATCH = 1
SEQ = 16384
DEPTH = 2
DEC_BATCH = 32
DEC_SEQ = 4
PAST_LEN = 16384
PAGE_SIZE = 128

N_MIXERS = 2
N_ATTN_LAYERS = (DEPTH + N_MIXERS - 1) // N_MIXERS
N_SSM_LAYERS = DEPTH // N_MIXERS
N_HEADS = 16
HEAD_DIM = D_MODEL // N_HEADS
BLOCK = 256
TOPK = 3
Q_CHUNK = 128
N_BUCKETS = 32
MAX_EXACT = N_BUCKETS // 2
MAX_DISTANCE = 4096
GROUP_P = 16
N_GROUPS = D_MODEL // GROUP_P
STATE_N = 64
D_FF = ((8 * D_MODEL // 3 + 127) // 128) * 128
PLE_DIM = 256
ALPHA = (2 * DEPTH) ** 0.25
BETA = (8 * DEPTH) ** -0.25
LN_EPS = 1e-5
NEG_INF = -1e30

kernel_name = 'moba_s5_macaron_deepnorm_step'


def layer_norm(x, g, b):
    xf = x.astype(jnp.float32)
    mu = jnp.mean(xf, axis=-1, keepdims=True)
    var = jnp.mean(jnp.square(xf - mu), axis=-1, keepdims=True)
    return ((xf - mu) * lax.rsqrt(var + LN_EPS) * g + b).astype(x.dtype)


def half_ffn(x, w1, w3, w2):
    return 0.5 * ((jax.nn.silu(x @ w1) * (x @ w3)) @ w2)


def rel_bucket(dist):
    n = jnp.maximum(dist, 0)
    nf = jnp.maximum(n, 1).astype(jnp.float32)
    log_b = MAX_EXACT + (jnp.log(nf / MAX_EXACT) / math.log(MAX_DISTANCE / MAX_EXACT)
                         * (N_BUCKETS - MAX_EXACT)).astype(jnp.int32)
    return jnp.where(n < MAX_EXACT, n, jnp.minimum(log_b, N_BUCKETS - 1))


def bias_gathered(table, dist):
    hi = jnp.arange(N_HEADS)[None, :, None, None]
    return table.T[hi, rel_bucket(dist)].astype(jnp.float32)


def bias_own(table, dist):
    return jnp.transpose(table[rel_bucket(dist)], (2, 0, 1))[None].astype(jnp.float32)


def block_softmax_attend(s_sel, v_sel, s_own, v_own):
    if s_sel is None:
        p = jax.nn.softmax(s_own, axis=-1)
        return jnp.einsum('bhqk,bhkd->bhqd', p.astype(v_own.dtype), v_own)
    n_sel = s_sel.shape[-1]
    p = jax.nn.softmax(jnp.concatenate([s_sel, s_own], axis=-1), axis=-1)
    return (jnp.einsum('bhqk,bhqkd->bhqd', p[..., :n_sel].astype(v_sel.dtype), v_sel)
            + jnp.einsum('bhqk,bhkd->bhqd', p[..., n_sel:].astype(v_own.dtype), v_own))


def qkv_heads(x, w_qkv):
    b, s, _ = x.shape
    q, k, v = jnp.split(x @ w_qkv, 3, axis=-1)
    return [t.reshape(b, s, N_HEADS, HEAD_DIM).transpose(0, 2, 1, 3) for t in (q, k, v)]


def moba_prompt(x, w_qkv, w_o, table):
    b, s, _ = x.shape
    q, k, v = qkv_heads(x, w_qkv)
    nb = -(-s // BLOCK)
    pad = nb * BLOCK - s
    kb = jnp.pad(k, ((0, 0), (0, 0), (0, pad), (0, 0))).reshape(b, N_HEADS, nb, BLOCK, HEAD_DIM)
    vb = jnp.pad(v, ((0, 0), (0, 0), (0, pad), (0, 0))).reshape(b, N_HEADS, nb, BLOCK, HEAD_DIM)
    kmean = jnp.mean(kb, axis=3, dtype=jnp.float32)
    k_sel = max(1, min(TOPK, nb - 1))
    bi = jnp.arange(b)[:, None, None, None]
    hi = jnp.arange(N_HEADS)[None, :, None, None]
    scale = HEAD_DIM ** -0.5

    def chunk(c):
        start = c * Q_CHUNK
        qc = lax.dynamic_slice_in_dim(q, start, Q_CHUNK, axis=2)
        q_pos = start + jnp.arange(Q_CHUNK)
        own = start // BLOCK
        gate = jnp.einsum('bhqd,bhnd->bhqn', qc.astype(jnp.float32), kmean)
        gate = jnp.where(jnp.arange(nb) < own, gate, NEG_INF)
        _, idx = lax.top_k(gate, k_sel)
        valid = jnp.repeat(idx < own, BLOCK, axis=-1)
        k_g = kb[bi, hi, idx].reshape(b, N_HEADS, Q_CHUNK, k_sel * BLOCK, HEAD_DIM)
        v_g = vb[bi, hi, idx].reshape(b, N_HEADS, Q_CHUNK, k_sel * BLOCK, HEAD_DIM)
        k_pos = (idx[..., None] * BLOCK + jnp.arange(BLOCK)).reshape(b, N_HEADS, Q_CHUNK, k_sel * BLOCK)
        s_sel = (jnp.einsum('bhqd,bhqkd->bhqk', qc, k_g).astype(jnp.float32) * scale
                 + bias_gathered(table, q_pos[:, None] - k_pos))
        s_sel = jnp.where(valid, s_sel, NEG_INF)
        k_o = lax.dynamic_slice_in_dim(kb, own, 1, axis=2)[:, :, 0]
        v_o = lax.dynamic_slice_in_dim(vb, own, 1, axis=2)[:, :, 0]
        dist_o = q_pos[:, None] - (own * BLOCK + jnp.arange(BLOCK))[None, :]
        s_own = (jnp.einsum('bhqd,bhkd->bhqk', qc, k_o).astype(jnp.float32) * scale
                 + bias_own(table, dist_o))
        s_own = jnp.where(dist_o >= 0, s_own, NEG_INF)
        return block_softmax_attend(s_sel, v_g, s_own, v_o)

    o = lax.map(chunk, jnp.arange(s // Q_CHUNK))
    o = jnp.transpose(o, (1, 0, 3, 2, 4)).reshape(b, s, D_MODEL)
    return o @ w_o, k, v


def moba_sample(x, cache_k, cache_v, page_table, li, w_qkv, w_o, table):
    db, nq, _ = x.shape
    q, k, v = qkv_heads(x, w_qkv)
    q_pos = PAST_LEN + jnp.arange(nq)
    n_pages = PAST_LEN // PAGE_SIZE
    ppb = BLOCK // PAGE_SIZE
    nb_full = PAST_LEN // BLOCK
    own_pages = n_pages - nb_full * ppb
    scale = HEAD_DIM ** -0.5
    s_sel, v_g = None, None
    if nb_full > 0:
        page_mean = jnp.mean(cache_k, axis=3, dtype=jnp.float32)[li]
        kmean = page_mean[page_table[:, :nb_full * ppb]]
        kmean = kmean.reshape(db, nb_full, ppb, N_HEADS, HEAD_DIM).mean(axis=2).transpose(0, 2, 1, 3)
        k_sel = min(TOPK, nb_full)
        gate = jnp.einsum('bhqd,bhnd->bhqn', q.astype(jnp.float32), kmean)
        _, idx = lax.top_k(gate, k_sel)
        lp = idx[..., None] * ppb + jnp.arange(ppb)
        phys = page_table[jnp.arange(db)[:, None, None, None, None], lp]
        hi5 = jnp.arange(N_HEADS)[None, :, None, None, None]
        k_g = cache_k[li, phys, hi5].reshape(db, N_HEADS, nq, k_sel * BLOCK, HEAD_DIM)
        v_g = cache_v[li, phys, hi5].reshape(db, N_HEADS, nq, k_sel * BLOCK, HEAD_DIM)
        k_pos = (idx[..., None] * BLOCK + jnp.arange(BLOCK)).reshape(db, N_HEADS, nq, k_sel * BLOCK)
        s_sel = (jnp.einsum('bhqd,bhqkd->bhqk', q, k_g).astype(jnp.float32) * scale
                 + bias_gathered(table, q_pos[:, None] - k_pos))
    if own_pages > 0:
        phys_o = page_table[:, nb_full * ppb:]
        k_past = cache_k[li, phys_o].transpose(0, 2, 1, 3, 4).reshape(db, N_HEADS, own_pages * PAGE_SIZE, HEAD_DIM)
        v_past = cache_v[li, phys_o].transpose(0, 2, 1, 3, 4).reshape(db, N_HEADS, own_pages * PAGE_SIZE, HEAD_DIM)
        k_o = jnp.concatenate([k_past, k], axis=2)
        v_o = jnp.concatenate([v_past, v], axis=2)
    else:
        k_o, v_o = k, v
    dist_o = q_pos[:, None] - (nb_full * BLOCK + jnp.arange(k_o.shape[2]))[None, :]
    s_own = (jnp.einsum('bhqd,bhkd->bhqk', q, k_o).astype(jnp.float32) * scale
             + bias_own(table, dist_o))
    s_own = jnp.where(dist_o >= 0, s_own, NEG_INF)
    o = block_softmax_attend(s_sel, v_g, s_own, v_o)
    o = o.transpose(0, 2, 1, 3).reshape(db, nq, D_MODEL)
    return o @ w_o, k, v


def complex_affine_combine(e1, e2):
    a1r, a1i, b1r, b1i = e1
    a2r, a2i, b2r, b2i = e2
    return (a2r * a1r - a2i * a1i, a2r * a1i + a2i * a1r,
            a2r * b1r - a2i * b1i + b2r, a2r * b1i + a2i * b1r + b2i)


def s5_mixer(x, s0_re, s0_im, w_in, lam_re, lam_im, log_dt, b_re, b_im, c_re, c_im, d, w_glu, b_glu, w_out):
    bsz, s, _ = x.shape
    u = (x @ w_in).astype(jnp.float32)
    ug = u.reshape(bsz, s, N_GROUPS, GROUP_P)
    lr = lam_re.astype(jnp.float32)
    lim = lam_im.astype(jnp.float32)
    dt = jnp.exp(log_dt.astype(jnp.float32))[:, None]
    mag = jnp.exp(lr * dt)
    ang = lim * dt
    ab_re, ab_im = mag * jnp.cos(ang), mag * jnp.sin(ang)
    den = lr * lr + lim * lim
    f_re = ((ab_re - 1.0) * lr + ab_im * lim) / den
    f_im = (ab_im * lr - (ab_re - 1.0) * lim) / den
    bb_re = f_re[..., None] * b_re - f_im[..., None] * b_im
    bb_im = f_re[..., None] * b_im + f_im[..., None] * b_re
    bu_re = jnp.einsum('bsgp,gnp->bsgn', ug, bb_re)
    bu_im = jnp.einsum('bsgp,gnp->bsgn', ug, bb_im)
    a_re = jnp.broadcast_to(ab_re, bu_re.shape)
    a_im = jnp.broadcast_to(ab_im, bu_im.shape)
    acc_re, acc_im, h_re, h_im = lax.associative_scan(
        complex_affine_combine, (a_re, a_im, bu_re, bu_im), axis=1)
    s0r = s0_re.astype(jnp.float32)[:, None]
    s0i = s0_im.astype(jnp.float32)[:, None]
    st_re = h_re + acc_re * s0r - acc_im * s0i
    st_im = h_im + acc_re * s0i + acc_im * s0r
    y = (jnp.einsum('gpn,bsgn->bsgp', c_re, st_re)
         - jnp.einsum('gpn,bsgn->bsgp', c_im, st_im)).reshape(bsz, s, D_MODEL)
    y = (y + d * u).astype(x.dtype)
    z = jax.nn.gelu(y) * jax.nn.sigmoid(y @ w_glu + b_glu)
    return z @ w_out, st_re[:, -1], st_im[:, -1]


def _normal(key, shape, scale):
    return scale * jax.random.normal(key, shape, jnp.float32)


def setup_inputs(seed: int = 0) -> dict:
    key = jax.random.key(seed)
    ks = jax.random.split(key, 40)
    n_pages = PAST_LEN // PAGE_SIZE
    n_used = DEC_BATCH * n_pages
    n_pool = n_used + n_used // 4
    page_table = jax.random.permutation(ks[0], n_pool)[:n_used].reshape(DEC_BATCH, n_pages).astype(jnp.int32)
    d_sc = D_MODEL ** -0.5
    return {
        'x_prompt': _normal(ks[1], (BATCH, SEQ, D_MODEL), 1.0),
        'x_sample': _normal(ks[2], (DEC_BATCH, DEC_SEQ, D_MODEL), 1.0),
        'cache_k': _normal(ks[3], (N_ATTN_LAYERS, n_pool, N_HEADS, PAGE_SIZE, HEAD_DIM), 1.0),
        'cache_v': _normal(ks[4], (N_ATTN_LAYERS, n_pool, N_HEADS, PAGE_SIZE, HEAD_DIM), 1.0),
        'state_s5_re': _normal(ks[5], (N_SSM_LAYERS, DEC_BATCH, N_GROUPS, STATE_N), 0.5),
        'state_s5_im': _normal(ks[6], (N_SSM_LAYERS, DEC_BATCH, N_GROUPS, STATE_N), 0.5),
        'page_table': page_table,
        'p_prompt': _normal(ks[7], (DEPTH, BATCH, SEQ, PLE_DIM), 1.0),
        'p_sample': _normal(ks[8], (DEPTH, DEC_BATCH, DEC_SEQ, PLE_DIM), 1.0),
        'ln_g': 1.0 + _normal(ks[9], (DEPTH, 3, D_MODEL), 0.02),
        'ln_b': _normal(ks[10], (DEPTH, 3, D_MODEL), 0.02),
        'ffn_w1': _normal(ks[11], (DEPTH, 2, D_MODEL, D_FF), d_sc),
        'ffn_w3': _normal(ks[12], (DEPTH, 2, D_MODEL, D_FF), d_sc),
        'ffn_w2': _normal(ks[13], (DEPTH, 2, D_FF, D_MODEL), BETA * D_FF ** -0.5),
        'ple_w_proj': _normal(ks[14], (DEPTH, PLE_DIM, D_MODEL), 0.5 * PLE_DIM ** -0.5),
        'ple_w_gate': _normal(ks[15], (DEPTH, D_MODEL, D_MODEL), d_sc),
        'rel_bias_table': _normal(ks[16], (N_BUCKETS, N_HEADS), 0.5),
        'attn_w_qkv': _normal(ks[17], (N_ATTN_LAYERS, D_MODEL, 3 * D_MODEL), d_sc),
        'attn_w_o': _normal(ks[18], (N_ATTN_LAYERS, D_MODEL, D_MODEL), BETA * d_sc),
        's5_w_in': _normal(ks[19], (N_SSM_LAYERS, D_MODEL, D_MODEL), d_sc),
        's5_lambda_re': -0.5 + _normal(ks[20], (N_SSM_LAYERS, N_GROUPS, STATE_N), 0.01),
        's5_lambda_im': jnp.broadcast_to(jnp.pi * jnp.arange(STATE_N, dtype=jnp.float32),
                                         (N_SSM_LAYERS, N_GROUPS, STATE_N)),
        's5_log_dt': jax.random.uniform(ks[21], (N_SSM_LAYERS, N_GROUPS), jnp.float32,
                                        math.log(0.001), math.log(0.1)),
        's5_b_re': _normal(ks[22], (N_SSM_LAYERS, N_GROUPS, STATE_N, GROUP_P), (2 * GROUP_P) ** -0.5),
        's5_b_im': _normal(ks[23], (N_SSM_LAYERS, N_GROUPS, STATE_N, GROUP_P), (2 * GROUP_P) ** -0.5),
        's5_c_re': _normal(ks[24], (N_SSM_LAYERS, N_GROUPS, GROUP_P, STATE_N), (2 * STATE_N) ** -0.5),
        's5_c_im': _normal(ks[25], (N_SSM_LAYERS, N_GROUPS, GROUP_P, STATE_N), (2 * STATE_N) ** -0.5),
        's5_d': _normal(ks[26], (N_SSM_LAYERS, D_MODEL), 1.0),
        's5_w_glu': _normal(ks[27], (N_SSM_LAYERS, D_MODEL, D_MODEL), d_sc),
        's5_b_glu': _normal(ks[28], (N_SSM_LAYERS, D_MODEL), 0.02),
        's5_w_out': _normal(ks[29], (N_SSM_LAYERS, D_MODEL, D_MODEL), BETA * d_sc),
    }


def reference(x_prompt, x_sample, cache_k, cache_v, state_s5_re, state_s5_im, page_table,
              p_prompt, p_sample, ln_g, ln_b, ffn_w1, ffn_w3, ffn_w2, ple_w_proj, ple_w_gate,
              rel_bias_table, attn_w_qkv, attn_w_o, s5_w_in, s5_lambda_re, s5_lambda_im, s5_log_dt,
              s5_b_re, s5_b_im, s5_c_re, s5_c_im, s5_d, s5_w_glu, s5_b_glu, s5_w_out):
    xp, xs = x_prompt, x_sample
    kp_l, vp_l, ks_l, vs_l = [], [], [], []
    spr_l, spi_l, ssr_l, ssi_l = [], [], [], []
    for i in range(DEPTH):
        li = i // N_MIXERS
        xp = layer_norm(ALPHA * xp + half_ffn(xp, ffn_w1[i, 0], ffn_w3[i, 0], ffn_w2[i, 0]), ln_g[i, 0], ln_b[i, 0])
        xs = layer_norm(ALPHA * xs + half_ffn(xs, ffn_w1[i, 0], ffn_w3[i, 0], ffn_w2[i, 0]), ln_g[i, 0], ln_b[i, 0])
        if i % N_MIXERS == 0:
            mp, kp, vp = moba_prompt(xp, attn_w_qkv[li], attn_w_o[li], rel_bias_table)
            ms, ks_, vs_ = moba_sample(xs, cache_k, cache_v, page_table, li,
                                       attn_w_qkv[li], attn_w_o[li], rel_bias_table)
            kp_l.append(kp)
            vp_l.append(vp)
            ks_l.append(ks_)
            vs_l.append(vs_)
        else:
            s5w = (s5_w_in[li], s5_lambda_re[li], s5_lambda_im[li], s5_log_dt[li], s5_b_re[li], s5_b_im[li],
                   s5_c_re[li], s5_c_im[li], s5_d[li], s5_w_glu[li], s5_b_glu[li], s5_w_out[li])
            zeros = jnp.zeros((xp.shape[0], N_GROUPS, STATE_N), jnp.float32)
            mp, spr, spi = s5_mixer(xp, zeros, zeros, *s5w)
            ms, ssr, ssi = s5_mixer(xs, state_s5_re[li], state_s5_im[li], *s5w)
            spr_l.append(spr)
            spi_l.append(spi)
            ssr_l.append(ssr)
            ssi_l.append(ssi)
        xp = layer_norm(ALPHA * xp + mp, ln_g[i, 1], ln_b[i, 1])
        xs = layer_norm(ALPHA * xs + ms, ln_g[i, 1], ln_b[i, 1])
        xp = layer_norm(ALPHA * xp + half_ffn(xp, ffn_w1[i, 1], ffn_w3[i, 1], ffn_w2[i, 1]), ln_g[i, 2], ln_b[i, 2])
        xs = layer_norm(ALPHA * xs + half_ffn(xs, ffn_w1[i, 1], ffn_w3[i, 1], ffn_w2[i, 1]), ln_g[i, 2], ln_b[i, 2])
        xp = xp + jax.nn.sigmoid(xp @ ple_w_gate[i]) * (p_prompt[i] @ ple_w_proj[i])
        xs = xs + jax.nn.sigmoid(xs @ ple_w_gate[i]) * (p_sample[i] @ ple_w_proj[i])
    return (xp, xs, jnp.stack(kp_l), jnp.stack(vp_l), jnp.stack(ks_l), jnp.stack(vs_l),
            jnp.stack(spr_l), jnp.stack(spi_l), jnp.stack(ssr_l), jnp.stack(ssi_l))
```

```python
import functools
import math

import jax
import jax.numpy as jnp
from jax import lax
from jax.experimental import pallas as pl
from jax.experimental.pallas import tpu as pltpu

F32 = jnp.float32
BF16 = jnp.bfloat16
I32 = jnp.int32

N_HEADS = 16
HEAD_DIM = 64
BLOCK = 256
TOPK = 3
PAGE_SIZE = 128
N_BUCKETS = 32
MAX_EXACT = N_BUCKETS // 2
MAX_DISTANCE = 4096
GROUP_P = 16
STATE_N = 64
LN_EPS = 1e-5
NEG_INF = -1e30
QK_SCALE = HEAD_DIM ** -0.5

LANES = 128
VMEM_LIMIT = 56 * 1024 * 1024

N_NEAR = MAX_DISTANCE // BLOCK + 1
S5_SLAB = 16


def _params(*sem):
    return pltpu.CompilerParams(dimension_semantics=sem, vmem_limit_bytes=VMEM_LIMIT)


def _resident(shape):
    nd = len(shape)
    return pl.BlockSpec(shape, lambda *_: (0,) * nd, pipeline_mode=pl.Buffered(1))


def _dot(a, b):
    return jnp.dot(a, b, preferred_element_type=F32)


def _layer_norm(y, g, b):
    mu = jnp.mean(y, axis=-1, keepdims=True)
    yc = y - mu
    var = jnp.mean(yc * yc, axis=-1, keepdims=True)
    return yc * lax.rsqrt(var + LN_EPS) * g + b


def _top3(work, cols, axis):
    out = []
    big = jnp.int32(1 << 30)
    for _ in range(TOPK):
        mx = jnp.max(work, axis=axis, keepdims=True)
        am = jnp.min(jnp.where(work == mx, cols, big), axis=axis, keepdims=True)
        out.append(jnp.where(mx > -jnp.inf, am, -1))
        work = jnp.where(cols == am, -jnp.inf, work)
    return out


def _bias_of(dist, head, thr_ref, tab_ref):
    def step(b, acc):
        return jnp.where(dist >= thr_ref[b], tab_ref[b * N_HEADS + head], acc)
    init = jnp.full(dist.shape, tab_ref[head], F32)
    return lax.fori_loop(1, N_BUCKETS, step, init)


def _ffn_body(x_ref, w1_ref, w3_ref, w2_ref, g_ref, b_ref, *rest, alpha, tf, ple):
    if ple:
        p_ref, wg_ref, wp_ref, o_ref = rest
    else:
        (o_ref,) = rest
    x = x_ref[...]
    xb = x.astype(BF16)
    acc = jnp.zeros(x.shape, F32)
    for j in range(w1_ref.shape[1] // tf):
        sl = slice(j * tf, (j + 1) * tf)
        h = jax.nn.silu(_dot(xb, w1_ref[:, sl])) * _dot(xb, w3_ref[:, sl])
        acc = acc + _dot(h.astype(BF16), w2_ref[sl, :])
    y = _layer_norm(alpha * x + 0.5 * acc, g_ref[...], b_ref[...])
    if ple:
        gate = jax.nn.sigmoid(_dot(y.astype(BF16), wg_ref[...]))
        y = y + gate * _dot(p_ref[...].astype(BF16), wp_ref[...])
    o_ref[...] = y


def ffn_ln(x, w1, w3, w2, g, b, alpha, ple=None):
    m, d = x.shape
    tm = min(512, m)
    row = lambda i: (i, 0)
    in_specs = [pl.BlockSpec((tm, d), row), _resident(w1.shape), _resident(w3.shape), _resident(w2.shape),
                _resident(g.shape), _resident(b.shape)]
    args = [x, w1, w3, w2, g, b]
    if ple is not None:
        p, wg, wp = ple
        in_specs += [pl.BlockSpec((tm, p.shape[1]), row), _resident(wg.shape), _resident(wp.shape)]
        args += [p, wg, wp]
    return pl.pallas_call(
        functools.partial(_ffn_body, alpha=alpha, tf=2 * LANES, ple=ple is not None),
        grid=(m // tm,), in_specs=in_specs, out_specs=pl.BlockSpec((tm, d), row),
        out_shape=jax.ShapeDtypeStruct((m, d), F32), compiler_params=_params("parallel"),
        name="ffn_ln")(*args)


def _lin_ln_body(x_ref, a_ref, w_ref, g_ref, b_ref, o_ref, *, alpha):
    m = _dot(a_ref[...].astype(BF16), w_ref[...])
    o_ref[...] = _layer_norm(alpha * x_ref[...] + m, g_ref[...], b_ref[...])


def lin_ln(x, a, w, g, b, alpha):
    m, d = x.shape
    tm = min(512, m)
    row = lambda i: (i, 0)
    return pl.pallas_call(
        functools.partial(_lin_ln_body, alpha=alpha),
        grid=(m // tm,),
        in_specs=[pl.BlockSpec((tm, d), row), pl.BlockSpec((tm, a.shape[1]), row), _resident(w.shape),
                  _resident(g.shape), _resident(b.shape)],
        out_specs=pl.BlockSpec((tm, d), row), out_shape=jax.ShapeDtypeStruct((m, d), F32),
        compiler_params=_params("parallel"), name="lin_ln")(x, a, w, g, b)


def _linear_body(x_ref, w_ref, o_ref):
    o_ref[...] = _dot(x_ref[...].astype(BF16), w_ref[...])


def linear(x, w):
    m = x.shape[0]
    return pl.pallas_call(
        _linear_body, grid=(1,), in_specs=[_resident(x.shape), _resident(w.shape)],
        out_specs=pl.BlockSpec((m, w.shape[1]), lambda i: (0, 0)),
        out_shape=jax.ShapeDtypeStruct((m, w.shape[1]), F32), compiler_params=_params("arbitrary"),
        name="linear")(x, w)


def _qkv_body(x_ref, wq_ref, wk_ref, wv_ref, wkt_ref, q_ref, ko_ref, vo_ref, kt_ref, vb_ref, ks_ref):
    xb = x_ref[...].astype(BF16)
    tm, d = xb.shape
    q_ref[...] = _dot(xb, wq_ref[...])
    k = _dot(xb, wk_ref[...])
    v = _dot(xb, wv_ref[...])
    for h in range(N_HEADS):
        ko_ref[h] = k[:, h * HEAD_DIM:(h + 1) * HEAD_DIM]
        vo_ref[h] = v[:, h * HEAD_DIM:(h + 1) * HEAD_DIM]
    vb_ref[...] = v.astype(BF16)
    kt = lax.dot_general(wkt_ref[...], xb, (((1,), (1,)), ((), ())), preferred_element_type=F32).astype(BF16)
    for p in range(d // LANES):
        for bk in range(tm // BLOCK):
            kt_ref[p, bk] = kt[p * LANES:(p + 1) * LANES, bk * BLOCK:(bk + 1) * BLOCK]
    for bk in range(tm // BLOCK):
        ks_ref[0, bk:bk + 1, :] = jnp.sum(k[bk * BLOCK:(bk + 1) * BLOCK], axis=0, keepdims=True)


def qkv_prompt(x, wq, wk, wv, wkt):
    s, d = x.shape
    tm = 2 * BLOCK
    nb = s // BLOCK
    row = lambda i: (i, 0)
    return pl.pallas_call(
        _qkv_body, grid=(s // tm,),
        in_specs=[pl.BlockSpec((tm, d), row)] + [_resident(w.shape) for w in (wq, wk, wv, wkt)],
        out_specs=[pl.BlockSpec((tm, d), row),
                   pl.BlockSpec((N_HEADS, tm, HEAD_DIM), lambda i: (0, i, 0)),
                   pl.BlockSpec((N_HEADS, tm, HEAD_DIM), lambda i: (0, i, 0)),
                   pl.BlockSpec((d // LANES, tm // BLOCK, LANES, BLOCK), lambda i: (0, i, 0, 0)),
                   pl.BlockSpec((tm, d), row),
                   pl.BlockSpec((1, tm // BLOCK, d), lambda i: (i, 0, 0))],
        out_shape=[jax.ShapeDtypeStruct((s, d), F32),
                   jax.ShapeDtypeStruct((N_HEADS, s, HEAD_DIM), F32),
                   jax.ShapeDtypeStruct((N_HEADS, s, HEAD_DIM), F32),
                   jax.ShapeDtypeStruct((d // LANES, nb, LANES, BLOCK), BF16),
                   jax.ShapeDtypeStruct((s, d), BF16),
                   jax.ShapeDtypeStruct((s // tm, tm // BLOCK, d), F32)],
        compiler_params=_params("parallel"), name="qkv_prompt")(x, wq, wk, wv, wkt)


def _split_bf16(x):
    hi = x.astype(BF16)
    return hi, (x - hi.astype(F32)).astype(BF16)


def _moba_body(thr_ref, tab_ref, q_ref, kt_ref, v_ref, km_ref, o_ref, bias_ref):
    hp = pl.program_id(0)
    i = pl.program_id(1)
    nb = km_ref.shape[1]
    rr = lax.broadcasted_iota(I32, (BLOCK, BLOCK), 0)
    cc = lax.broadcasted_iota(I32, (BLOCK, BLOCK), 1)

    @pl.when(i == 0)
    def _():
        for hh in range(2):
            head = hp * 2 + hh

            def fill(dl, carry):
                bias_ref[hh, dl] = _bias_of(rr - cc + dl * BLOCK, head, thr_ref, tab_ref)
                return carry
            lax.fori_loop(0, N_NEAR, fill, 0)
            bias_ref[hh, N_NEAR] = jnp.full((BLOCK, BLOCK), tab_ref[(N_BUCKETS - 1) * N_HEADS + head], F32)

    lane = lax.broadcasted_iota(I32, (BLOCK, LANES), 1)
    bcol = lax.broadcasted_iota(I32, (BLOCK, nb), 1)
    q = q_ref[...]
    km_hi, km_lo = _split_bf16(km_ref[...])
    outs = []
    for hh in range(2):
        qh = jnp.where((lane >= hh * HEAD_DIM) & (lane < (hh + 1) * HEAD_DIM), q, 0.0)
        q_hi, q_lo = _split_bf16(qh)
        gate = _dot(q_hi, km_hi) + _dot(q_hi, km_lo) + _dot(q_lo, km_hi)
        i1, i2, i3 = _top3(jnp.where(bcol < i, gate, -jnp.inf), bcol, 1)
        qs = (qh * QK_SCALE).astype(BF16)

        s = _dot(qs, kt_ref[0, i]) + bias_ref[hh, 0]
        s = jnp.where(rr >= cc, s, NEG_INF)
        m = jnp.max(s, axis=-1, keepdims=True)
        p = jnp.exp(s - m)
        l = jnp.sum(p, axis=-1, keepdims=True)
        acc = _dot(p.astype(BF16), v_ref[pl.ds(pl.multiple_of(i * BLOCK, BLOCK), BLOCK), :])

        def body(n, carry):
            m, l, acc = carry
            s = _dot(qs, kt_ref[0, n]) + bias_ref[hh, jnp.minimum(i - n, N_NEAR)]
            s = jnp.where((i1 == n) | (i2 == n) | (i3 == n), s, NEG_INF)
            m_new = jnp.maximum(m, jnp.max(s, axis=-1, keepdims=True))
            a = jnp.exp(m - m_new)
            p = jnp.exp(s - m_new)
            l = a * l + jnp.sum(p, axis=-1, keepdims=True)
            acc = a * acc + _dot(p.astype(BF16), v_ref[pl.ds(pl.multiple_of(n * BLOCK, BLOCK), BLOCK), :])
            return m_new, l, acc

        m, l, acc = lax.fori_loop(0, i, body, (m, l, acc))
        outs.append(acc / l)
    o_ref[...] = jnp.where(lane < HEAD_DIM, outs[0], outs[1]).astype(BF16)


def moba_prompt(q, kt, vb, kmt, thr, tab):
    s, d = q.shape
    nb = s // BLOCK
    smem = pl.BlockSpec(memory_space=pltpu.SMEM)
    return pl.pallas_call(
        _moba_body, grid=(d // LANES, nb),
        in_specs=[smem, smem,
                  pl.BlockSpec((BLOCK, LANES), lambda hp, i: (i, hp)),
                  pl.BlockSpec((1, nb, LANES, BLOCK), lambda hp, i: (hp, 0, 0, 0)),
                  pl.BlockSpec((s, LANES), lambda hp, i: (0, hp)),
                  pl.BlockSpec((LANES, nb), lambda hp, i: (hp, 0))],
        out_specs=pl.BlockSpec((BLOCK, LANES), lambda hp, i: (i, hp)),
        out_shape=jax.ShapeDtypeStruct((s, d), BF16),
        scratch_shapes=[pltpu.VMEM((2, N_NEAR + 1, BLOCK, BLOCK), F32)],
        compiler_params=_params("arbitrary", "arbitrary"), name="moba_prompt")(thr, tab, q, kt, vb, kmt)


def _s5_body(x_ref, win_ref, bbr_ref, bbi_ref, cr_ref, ci_ref, ar_ref, ai_ref, d_ref, wglu_ref, bglu_ref,
             wout_ref, g_ref, b_ref, h0r_ref, h0i_ref, o_ref, hr_ref, hi_ref,
             u_ref, y_ref, sr_ref, si_ref, *chain_scratch, alpha, nseg, lseg, cw, chained):
    step = pl.program_id(0)
    slab = S5_SLAB * GROUP_P
    sw = S5_SLAB * STATE_N
    n_slab = win_ref.shape[1] // slab
    rows = lambda j: pl.ds(j, nseg, stride=lseg)

    if chained:
        pwr_ref, pwi_ref, cyr_ref, cyi_ref = chain_scratch

        @pl.when(step == 0)
        def _():
            cyr_ref[...] = h0r_ref[...]
            cyi_ref[...] = h0i_ref[...]
            for c in range(ar_ref.shape[1] // sw):
                cs = slice(c * sw, (c + 1) * sw)
                ar, ai = ar_ref[:, cs], ai_ref[:, cs]

                def power(j, carry):
                    pr, pi = carry
                    pwr_ref[pl.ds(j, 1), cs] = pr
                    pwi_ref[pl.ds(j, 1), cs] = pi
                    return pr * ar - pi * ai, pr * ai + pi * ar
                lax.fori_loop(0, lseg, power, (ar, ai))

    n_lt = cw // LANES

    def put(ref, val, t0=0, sel=slice(None)):
        for k in range(val.shape[1] // LANES):
            ref[t0 + k, sel, :] = val[:, k * LANES:(k + 1) * LANES]

    def get(ref, t0, n, sel=slice(None)):
        return jnp.concatenate([ref[t0 + k, sel, :] for k in range(n)], axis=1)

    x = x_ref[...]
    u_ref[...] = _dot(x.astype(BF16), win_ref[...])
    for s in range(n_slab):
        ub = u_ref[:, s * slab:(s + 1) * slab].astype(BF16)
        put(sr_ref, _dot(ub, bbr_ref[s]))
        put(si_ref, _dot(ub, bbi_ref[s]))
        for c in range(sw // cw):
            t0 = c * n_lt
            gs = slice(s * sw + c * cw, s * sw + (c + 1) * cw)
            ar = jnp.broadcast_to(ar_ref[:, gs], (nseg, cw))
            ai = jnp.broadcast_to(ai_ref[:, gs], (nseg, cw))

            def scan(j, carry):
                hr, hi = carry
                nr = ar * hr - ai * hi + get(sr_ref, t0, n_lt, rows(j))
                ni = ar * hi + ai * hr + get(si_ref, t0, n_lt, rows(j))
                put(sr_ref, nr, t0, rows(j))
                put(si_ref, ni, t0, rows(j))
                return nr, ni

            if not chained:
                er, ei = lax.fori_loop(0, lseg, scan, (h0r_ref[:, gs], h0i_ref[:, gs]))
                hr_ref[:, gs] = er
                hi_ref[:, gs] = ei
            else:
                zero = jnp.zeros((nseg, cw), F32)
                er, ei = lax.fori_loop(0, lseg, scan, (zero, zero))
                plr, pli = pwr_ref[lseg - 1:lseg, gs], pwi_ref[lseg - 1:lseg, gs]
                cr, ci = cyr_ref[:, gs], cyi_ref[:, gs]
                hsr, hsi = [], []
                for r in range(nseg):
                    hsr.append(cr)
                    hsi.append(ci)
                    cr, ci = (er[r:r + 1] + plr * cr - pli * ci, ei[r:r + 1] + plr * ci + pli * cr)
                cyr_ref[:, gs] = cr
                cyi_ref[:, gs] = ci
                hsr = jnp.concatenate(hsr, axis=0)
                hsi = jnp.concatenate(hsi, axis=0)

                def fix(j, carry):
                    pr = jnp.broadcast_to(pwr_ref[pl.ds(j, 1), gs], (nseg, cw))
                    pi = jnp.broadcast_to(pwi_ref[pl.ds(j, 1), gs], (nseg, cw))
                    put(sr_ref, get(sr_ref, t0, n_lt, rows(j)) + (pr * hsr - pi * hsi), t0, rows(j))
                    put(si_ref, get(si_ref, t0, n_lt, rows(j)) + (pr * hsi + pi * hsr), t0, rows(j))
                    return carry
                lax.fori_loop(0, lseg, fix, 0)
        n_st = sw // LANES
        ys = (_dot(get(sr_ref, 0, n_st).astype(BF16), cr_ref[s])
              - _dot(get(si_ref, 0, n_st).astype(BF16), ci_ref[s]))
        us = slice(s * slab, (s + 1) * slab)
        y_ref[:, us] = ys + d_ref[:, us] * u_ref[:, us]
    if chained:
        hr_ref[...] = cyr_ref[...]
        hi_ref[...] = cyi_ref[...]
    y = y_ref[...]
    z = jax.nn.gelu(y) * jax.nn.sigmoid(_dot(y.astype(BF16), wglu_ref[...]) + bglu_ref[...])
    m = _dot(z.astype(BF16), wout_ref[...])
    o_ref[...] = _layer_norm(alpha * x + m, g_ref[...], b_ref[...])


def s5_layer(x, h0r, h0i, w, g, b, alpha, *, nseg, lseg, cw, chained):
    m, d = x.shape
    tb = nseg * lseg
    n_state = h0r.shape[1]
    sw = S5_SLAB * STATE_N
    row = lambda i: (i, 0)
    hs = h0r.shape
    scratch = [pltpu.VMEM((tb, d), F32), pltpu.VMEM((tb, d), F32), pltpu.VMEM((sw // LANES, tb, LANES), F32), pltpu.VMEM((sw // LANES, tb, LANES), F32)]
    if chained:
        scratch += [pltpu.VMEM((lseg, n_state), F32), pltpu.VMEM((lseg, n_state), F32),
                    pltpu.VMEM((1, n_state), F32), pltpu.VMEM((1, n_state), F32)]
    else:
        assert m == tb
    weights = [w["w_in"], w["bb_re"], w["bb_im"], w["c_re"], w["c_im"], w["a_re"], w["a_im"], w["d"],
               w["w_glu"], w["b_glu"], w["w_out"], g, b, h0r, h0i]
    return pl.pallas_call(
        functools.partial(_s5_body, alpha=alpha, nseg=nseg, lseg=lseg, cw=cw, chained=chained),
        grid=(m // tb,),
        in_specs=[pl.BlockSpec((tb, d), row)] + [_resident(a.shape) for a in weights],
        out_specs=[pl.BlockSpec((tb, d), row), pl.BlockSpec(hs, lambda i: (0, 0)), pl.BlockSpec(hs, lambda i: (0, 0))],
        out_shape=[jax.ShapeDtypeStruct((m, d), F32), jax.ShapeDtypeStruct(hs, F32), jax.ShapeDtypeStruct(hs, F32)],
        scratch_shapes=scratch, compiler_params=_params("arbitrary"), name="s5_layer")(x, *weights)


def s5_weights(w_in, lam_re, lam_im, log_dt, b_re, b_im, c_re, c_im, d, w_glu, b_glu, w_out):
    dt = jnp.exp(log_dt)[:, None]
    mag = jnp.exp(lam_re * dt)
    ang = lam_im * dt
    ab_re, ab_im = mag * jnp.cos(ang), mag * jnp.sin(ang)
    den = lam_re * lam_re + lam_im * lam_im
    f_re = ((ab_re - 1.0) * lam_re + ab_im * lam_im) / den
    f_im = (ab_im * lam_re - (ab_re - 1.0) * lam_im) / den
    bb_re = f_re[..., None] * b_re - f_im[..., None] * b_im
    bb_im = f_re[..., None] * b_im + f_im[..., None] * b_re
    n_g = lam_re.shape[0]
    eye = jnp.eye(S5_SLAB, dtype=F32)

    def in_slabs(bb):
        t = jnp.transpose(bb, (0, 2, 1)).reshape(n_g // S5_SLAB, S5_SLAB, GROUP_P, STATE_N)
        return jnp.einsum("sgpn,gh->sgphn", t, eye).reshape(n_g // S5_SLAB, S5_SLAB * GROUP_P, S5_SLAB * STATE_N).astype(BF16)

    def out_slabs(c):
        t = jnp.transpose(c, (0, 2, 1)).reshape(n_g // S5_SLAB, S5_SLAB, STATE_N, GROUP_P)
        return jnp.einsum("sgnp,gh->sgnhp", t, eye).reshape(n_g // S5_SLAB, S5_SLAB * STATE_N, S5_SLAB * GROUP_P).astype(BF16)

    return dict(w_in=w_in.astype(BF16), bb_re=in_slabs(bb_re), bb_im=in_slabs(bb_im), c_re=out_slabs(c_re),
                c_im=out_slabs(c_im), a_re=ab_re.reshape(1, -1), a_im=ab_im.reshape(1, -1), d=d.reshape(1, -1),
                w_glu=w_glu.astype(BF16), b_glu=b_glu.reshape(1, -1), w_out=w_out.astype(BF16))


def _sgate_body(pt_ref, q_ref, *rest, n_pg):
    pages, idx_ref, km_ref = rest[:n_pg], rest[n_pg], rest[n_pg + 1]
    pc = pl.program_id(1)
    ppb = BLOCK // PAGE_SIZE
    for u in range(n_pg // ppb):
        tot = sum(jnp.sum(pages[u * ppb + t][0, 0], axis=1) for t in range(ppb))
        km_ref[pc * (n_pg // ppb) + u] = tot * (1.0 / BLOCK)

    @pl.when(pc == pl.num_programs(1) - 1)
    def _():
        km = km_ref[...]
        brow = lax.broadcasted_iota(I32, km.shape[:2], 0)
        for qi in range(q_ref.shape[1]):
            gate = jnp.sum(km * q_ref[0, qi][None], axis=-1)
            for t, am in enumerate(_top3(gate, brow, 0)):
                idx_ref[0, qi, t:t + 1, :] = am


def sample_gate(q4, cache_k, page_table, li):
    db, nq, nh, dh = q4.shape
    n_pages = page_table.shape[1]
    n_pg = 8
    page = lambda t: pl.BlockSpec((1, 1, nh, PAGE_SIZE, dh),
                                  lambda b, pc, pt: (li, pt[b * n_pages + pc * n_pg + t], 0, 0, 0))
    return pl.pallas_call(
        functools.partial(_sgate_body, n_pg=n_pg),
        grid_spec=pltpu.PrefetchScalarGridSpec(
            num_scalar_prefetch=1, grid=(db, n_pages // n_pg),
            in_specs=[pl.BlockSpec((1, nq, nh, dh), lambda b, pc, pt: (b, 0, 0, 0))] + [page(t) for t in range(n_pg)],
            out_specs=pl.BlockSpec((1, nq, TOPK, nh), lambda b, pc, pt: (b, 0, 0, 0)),
            scratch_shapes=[pltpu.VMEM((n_pages * PAGE_SIZE // BLOCK, nh, dh), F32)]),
        out_shape=jax.ShapeDtypeStruct((db, nq, TOPK, nh), I32),
        compiler_params=_params("arbitrary", "arbitrary"), name="sample_gate")(
            page_table.reshape(-1), q4, *([cache_k] * n_pg))


def _sattn_body(pt_ref, idx_ref, thr_ref, tab_ref, q_ref, kn_ref, vn_ref, *rest, nq, past_len):
    n_pg = nq * TOPK * (BLOCK // PAGE_SIZE)
    kpg, vpg, o_ref = rest[:n_pg], rest[n_pg:2 * n_pg], rest[2 * n_pg]
    b = pl.program_id(0)
    h = pl.program_id(1)
    ppb = BLOCK // PAGE_SIZE
    far = thr_ref[N_BUCKETS - 1]
    last = tab_ref[(N_BUCKETS - 1) * N_HEADS + h]
    kn, vn = kn_ref[0, 0], vn_ref[0, 0]
    off = lax.broadcasted_iota(I32, (PAGE_SIZE, 1), 0)
    own = lax.broadcasted_iota(I32, (nq, 1), 0)
    for qi in range(nq):
        qrow = q_ref[0, 0, qi:qi + 1, :] * QK_SCALE
        scores = []
        for t in range(TOPK):
            blk = idx_ref[((b * nq + qi) * TOPK + t) * N_HEADS + h]
            for pp in range(ppb):
                k = kpg[(qi * TOPK + t) * ppb + pp][0, 0, 0]
                dist = (past_len + qi) - (blk * BLOCK + pp * PAGE_SIZE + off)
                bias = lax.cond((past_len + qi) - (blk * BLOCK + (pp + 1) * PAGE_SIZE - 1) >= far,
                                lambda: jnp.full((PAGE_SIZE, 1), last, F32),
                                lambda dist=dist: _bias_of(dist, h, thr_ref, tab_ref))
                scores.append(jnp.sum(k * qrow, axis=-1, keepdims=True) + bias)
        d_own = qi - own
        s_own = jnp.sum(kn * qrow, axis=-1, keepdims=True) + _bias_of(d_own, h, thr_ref, tab_ref)
        scores.append(jnp.where(d_own >= 0, s_own, NEG_INF))
        m = functools.reduce(jnp.maximum, [jnp.max(s, axis=0, keepdims=True) for s in scores])
        ps = [jnp.exp(s - m) for s in scores]
        l = functools.reduce(jnp.add, [jnp.sum(p, axis=0, keepdims=True) for p in ps])
        vals = [vpg[j][0, 0, 0] for j in range(qi * TOPK * ppb, (qi + 1) * TOPK * ppb)] + [vn]
        o = functools.reduce(jnp.add, [jnp.sum((p / l) * v, axis=0, keepdims=True) for p, v in zip(ps, vals)])
        o_ref[0, 0, qi:qi + 1, :] = o


def sample_attn(q, kn, vn, cache_k, cache_v, page_table, idx, thr, tab, li, past_len):
    db, nh, nq, dh = q.shape
    n_pages = page_table.shape[1]
    ppb = BLOCK // PAGE_SIZE

    def page(qi, t, pp):
        def index(b, h, pt, ix, th):
            blk = ix[((b * nq + qi) * TOPK + t) * nh + h]
            return (li, pt[b * n_pages + blk * ppb + pp], h, 0, 0)
        return pl.BlockSpec((1, 1, 1, PAGE_SIZE, dh), index)

    pages = [page(qi, t, pp) for qi in range(nq) for t in range(TOPK) for pp in range(ppb)]
    tok = pl.BlockSpec((1, 1, nq, dh), lambda b, h, pt, ix, th: (b, h, 0, 0))
    return pl.pallas_call(
        functools.partial(_sattn_body, nq=nq, past_len=past_len),
        grid_spec=pltpu.PrefetchScalarGridSpec(
            num_scalar_prefetch=3, grid=(db, nh),
            in_specs=[pl.BlockSpec(memory_space=pltpu.SMEM), tok, tok, tok] + pages + pages,
            out_specs=tok),
        out_shape=jax.ShapeDtypeStruct((db, nh, nq, dh), F32),
        compiler_params=_params("arbitrary", "arbitrary"), name="sample_attn")(
            page_table.reshape(-1), idx.reshape(-1), thr, tab, q, kn, vn,
            *([cache_k] * len(pages)), *([cache_v] * len(pages)))


def _bucket_thresholds():
    n = jnp.arange(MAX_DISTANCE + 1, dtype=I32)
    nf = jnp.maximum(n, 1).astype(F32)
    log_b = MAX_EXACT + (jnp.log(nf / MAX_EXACT) / math.log(MAX_DISTANCE / MAX_EXACT)
                         * (N_BUCKETS - MAX_EXACT)).astype(I32)
    bucket = jnp.where(n < MAX_EXACT, n, jnp.minimum(log_b, N_BUCKETS - 1))
    return jnp.sum(bucket[None, :] < jnp.arange(N_BUCKETS, dtype=I32)[:, None], axis=1).astype(I32)


def _heads(t, b, s):
    return t.reshape(b, s, N_HEADS, HEAD_DIM).transpose(0, 2, 1, 3)


def kernel(x_prompt, x_sample, cache_k, cache_v, state_s5_re, state_s5_im, page_table, p_prompt, p_sample, ln_g, ln_b, ffn_w1, ffn_w3, ffn_w2, ple_w_proj, ple_w_gate, rel_bias_table, attn_w_qkv, attn_w_o, s5_w_in, s5_lambda_re, s5_lambda_im, s5_log_dt, s5_b_re, s5_b_im, s5_c_re, s5_c_im, s5_d, s5_w_glu, s5_b_glu, s5_w_out):
    depth = ln_g.shape[0]
    alpha = (2 * depth) ** 0.25
    bsz, seq, d = x_prompt.shape
    db, nq, _ = x_sample.shape
    assert bsz == 1 and seq % (2 * BLOCK) == 0 and d == N_HEADS * HEAD_DIM
    past_len = page_table.shape[1] * PAGE_SIZE
    assert past_len % BLOCK == 0
    xp = x_prompt.reshape(seq, d)
    xs = x_sample.reshape(db * nq, d)
    thr = _bucket_thresholds()
    tab = rel_bias_table.reshape(-1)
    vec = lambda a: a.reshape(1, -1)
    outs = {k: [] for k in ("kp", "vp", "ks", "vs", "spr", "spi", "ssr", "ssi")}
    for i in range(depth):
        li = i // 2
        w1, w3, w2 = ffn_w1[i].astype(BF16), ffn_w3[i].astype(BF16), ffn_w2[i].astype(BF16)
        g, b = ln_g[i], ln_b[i]
        xp = ffn_ln(xp, w1[0], w3[0], w2[0], vec(g[0]), vec(b[0]), alpha)
        xs = ffn_ln(xs, w1[0], w3[0], w2[0], vec(g[0]), vec(b[0]), alpha)
        if i % 2 == 0:
            wqkv = attn_w_qkv[li].astype(BF16)
            wq, wk, wv = wqkv[:, :d], wqkv[:, d:2 * d], wqkv[:, 2 * d:]
            wo = attn_w_o[li].astype(BF16)
            q, ko, vo, kt, vb, ksum = qkv_prompt(xp, wq, wk, wv, wk.T)
            kmt = (ksum.reshape(seq // BLOCK, d) * (1.0 / BLOCK)).T
            o = moba_prompt(q, kt, vb, kmt, thr, tab)
            xp = lin_ln(xp, o, wo, vec(g[1]), vec(b[1]), alpha)
            outs["kp"].append(ko[None])
            outs["vp"].append(vo[None])

            qkv_s = linear(xs, wqkv)
            qs, ks, vs = (_heads(t, db, nq) for t in jnp.split(qkv_s, 3, axis=-1))
            idx = sample_gate(jnp.transpose(qs, (0, 2, 1, 3)), cache_k, page_table, li)
            o_s = sample_attn(qs, ks, vs, cache_k, cache_v, page_table, idx, thr, tab, li, past_len)
            xs = lin_ln(xs, o_s.transpose(0, 2, 1, 3).reshape(db * nq, d), wo, vec(g[1]), vec(b[1]), alpha)
            outs["ks"].append(ks)
            outs["vs"].append(vs)
        else:
            w = s5_weights(s5_w_in[li], s5_lambda_re[li], s5_lambda_im[li], s5_log_dt[li], s5_b_re[li], s5_b_im[li],
                           s5_c_re[li], s5_c_im[li], s5_d[li], s5_w_glu[li], s5_b_glu[li], s5_w_out[li])
            n_state = w["a_re"].shape[1]
            zero = jnp.zeros((1, n_state), F32)
            xp, hr, hi = s5_layer(xp, zero, zero, w, vec(g[1]), vec(b[1]), alpha, nseg=8, lseg=64, cw=1024, chained=True)
            outs["spr"].append(hr.reshape(1, -1, STATE_N))
            outs["spi"].append(hi.reshape(1, -1, STATE_N))
            xs, hr, hi = s5_layer(xs, state_s5_re[li].reshape(db, n_state), state_s5_im[li].reshape(db, n_state), w,
                                  vec(g[1]), vec(b[1]), alpha, nseg=db, lseg=nq, cw=256, chained=False)
            outs["ssr"].append(hr.reshape(db, -1, STATE_N))
            outs["ssi"].append(hi.reshape(db, -1, STATE_N))
        ple_w = (ple_w_gate[i].astype(BF16), ple_w_proj[i].astype(BF16))
        xp = ffn_ln(xp, w1[1], w3[1], w2[1], vec(g[2]), vec(b[2]), alpha, ple=(p_prompt[i].reshape(seq, -1),) + ple_w)
        xs = ffn_ln(xs, w1[1], w3[1], w2[1], vec(g[2]), vec(b[2]), alpha, ple=(p_sample[i].reshape(db * nq, -1),) + ple_w)
    st = lambda k: jnp.stack(outs[k])
    return (xp.reshape(bsz, seq, d), xs.reshape(db, nq, d), st("kp"), st("vp"), st("ks"), st("vs"),
            st("spr"), st("spi"), st("ssr"), st("ssi"))
```

```python
import functools
import math

import jax
import jax.numpy as jnp
from jax import lax
from jax.experimental import pallas as pl
from jax.experimental.pallas import tpu as pltpu

F32 = jnp.float32
BF16 = jnp.bfloat16
I32 = jnp.int32

N_HEADS = 16
HEAD_DIM = 64
BLOCK = 256
TOPK = 3
PAGE_SIZE = 128
N_BUCKETS = 32
MAX_EXACT = N_BUCKETS // 2
MAX_DISTANCE = 4096
GROUP_P = 16
STATE_N = 64
LN_EPS = 1e-5
NEG_INF = -1e30
MASK_BIG = 1e30
QK_SCALE = HEAD_DIM ** -0.5

LANES = 128
VMEM_LIMIT = 56 * 1024 * 1024

N_NEAR = MAX_DISTANCE // BLOCK + 1
Q_TILE = 2 * BLOCK
N_FAR_PAIR = (N_NEAR + 2) // 2
S5_SLAB = 16


def _params(*sem):
    return pltpu.CompilerParams(dimension_semantics=sem, vmem_limit_bytes=VMEM_LIMIT)


def _resident(shape):
    nd = len(shape)
    return pl.BlockSpec(shape, lambda *_: (0,) * nd, pipeline_mode=pl.Buffered(1))


def _dot(a, b):
    return jnp.dot(a, b, preferred_element_type=F32)


def _layer_norm(y, g, b):
    mu = jnp.mean(y, axis=-1, keepdims=True)
    yc = y - mu
    var = jnp.mean(yc * yc, axis=-1, keepdims=True)
    return yc * lax.rsqrt(var + LN_EPS) * g + b


def _top3(work, cols, axis):
    out = []
    big = jnp.int32(1 << 30)
    for _ in range(TOPK):
        mx = jnp.max(work, axis=axis, keepdims=True)
        am = jnp.min(jnp.where(work == mx, cols, big), axis=axis, keepdims=True)
        out.append(jnp.where(mx > -jnp.inf, am, -1))
        work = jnp.where(cols == am, -jnp.inf, work)
    return out


def _bias_of(dist, head, thr_ref, tab_ref, unroll=False):
    def step(b, acc):
        return jnp.where(dist >= thr_ref[b], tab_ref[b * N_HEADS + head], acc)
    acc = jnp.full(dist.shape, tab_ref[head], F32)
    if unroll:
        for b in range(1, N_BUCKETS):
            acc = step(b, acc)
        return acc
    return lax.fori_loop(1, N_BUCKETS, step, acc)


def _ffn_body(x_ref, w1_ref, w3_ref, w2_ref, g_ref, b_ref, *rest, alpha, tf, ple):
    if ple:
        p_ref, wg_ref, wp_ref, o_ref = rest
    else:
        (o_ref,) = rest
    x = x_ref[...]
    xb = x.astype(BF16)
    acc = jnp.zeros(x.shape, F32)
    for j in range(w1_ref.shape[1] // tf):
        sl = slice(j * tf, (j + 1) * tf)
        h = jax.nn.silu(_dot(xb, w1_ref[:, sl])) * _dot(xb, w3_ref[:, sl])
        acc = acc + _dot(h.astype(BF16), w2_ref[sl, :])
    y = _layer_norm(alpha * x + 0.5 * acc, g_ref[...], b_ref[...])
    if ple:
        gate = jax.nn.sigmoid(_dot(y.astype(BF16), wg_ref[...]))
        y = y + gate * _dot(p_ref[...].astype(BF16), wp_ref[...])
    o_ref[...] = y


def ffn_ln(x, w1, w3, w2, g, b, alpha, ple=None):
    m, d = x.shape
    tm = min(512, m)
    row = lambda i: (i, 0)
    in_specs = [pl.BlockSpec((tm, d), row), _resident(w1.shape), _resident(w3.shape), _resident(w2.shape),
                _resident(g.shape), _resident(b.shape)]
    args = [x, w1, w3, w2, g, b]
    if ple is not None:
        p, wg, wp = ple
        in_specs += [pl.BlockSpec((tm, p.shape[1]), row), _resident(wg.shape), _resident(wp.shape)]
        args += [p, wg, wp]
    return pl.pallas_call(
        functools.partial(_ffn_body, alpha=alpha, tf=2 * LANES, ple=ple is not None),
        grid=(m // tm,), in_specs=in_specs, out_specs=pl.BlockSpec((tm, d), row),
        out_shape=jax.ShapeDtypeStruct((m, d), F32), compiler_params=_params("parallel"),
        name="ffn_ln")(*args)


def _lin_ln_body(x_ref, a_ref, w_ref, g_ref, b_ref, o_ref, *, alpha):
    m = _dot(a_ref[...].astype(BF16), w_ref[...])
    o_ref[...] = _layer_norm(alpha * x_ref[...] + m, g_ref[...], b_ref[...])


def lin_ln(x, a, w, g, b, alpha):
    m, d = x.shape
    tm = min(512, m)
    row = lambda i: (i, 0)
    return pl.pallas_call(
        functools.partial(_lin_ln_body, alpha=alpha),
        grid=(m // tm,),
        in_specs=[pl.BlockSpec((tm, d), row), pl.BlockSpec((tm, a.shape[1]), row), _resident(w.shape),
                  _resident(g.shape), _resident(b.shape)],
        out_specs=pl.BlockSpec((tm, d), row), out_shape=jax.ShapeDtypeStruct((m, d), F32),
        compiler_params=_params("parallel"), name="lin_ln")(x, a, w, g, b)


def _linear_body(x_ref, w_ref, o_ref):
    o_ref[...] = _dot(x_ref[...].astype(BF16), w_ref[...])


def linear(x, w):
    m = x.shape[0]
    return pl.pallas_call(
        _linear_body, grid=(1,), in_specs=[_resident(x.shape), _resident(w.shape)],
        out_specs=pl.BlockSpec((m, w.shape[1]), lambda i: (0, 0)),
        out_shape=jax.ShapeDtypeStruct((m, w.shape[1]), F32), compiler_params=_params("arbitrary"),
        name="linear")(x, w)


def _qkv_body(x_ref, wq_ref, wv_ref, wkt_ref, wvt_ref, q_ref, kto_ref, vto_ref, kta_ref, vb_ref, km_ref):
    xb = x_ref[...].astype(BF16)
    tm, d = xb.shape
    nblk = tm // BLOCK
    q_ref[...] = _dot(xb, wq_ref[...])
    vb_ref[...] = _dot(xb, wv_ref[...]).astype(BF16)
    nt = lambda w_ref: lax.dot_general(w_ref[...], xb, (((1,), (1,)), ((), ())), preferred_element_type=F32)
    kt = nt(wkt_ref)
    kto_ref[...] = kt
    vto_ref[...] = nt(wvt_ref)
    ktb = kt.astype(BF16)
    row = lax.broadcasted_iota(I32, (LANES, BLOCK), 0)
    for bk in range(nblk):
        n = pl.program_id(0) * nblk + bk
        cols = slice(bk * BLOCK, (bk + 1) * BLOCK)
        km_ref[0, :, bk:bk + 1] = jnp.sum(kt[:, cols], axis=1, keepdims=True) * (1.0 / BLOCK)
        mask_rows = jnp.where(row % HEAD_DIM == n, MASK_BIG, 0.0).astype(BF16)
        for p in range(d // LANES):
            pair = ktb[p * LANES:(p + 1) * LANES, cols]
            kta_ref[2 * p, bk] = jnp.where(row < HEAD_DIM, pair, mask_rows)
            kta_ref[2 * p + 1, bk] = jnp.where(row >= HEAD_DIM, pair, mask_rows)


def qkv_prompt(x, wq, wv, wkt, wvt):
    s, d = x.shape
    tm = 2 * BLOCK
    nb = s // BLOCK
    assert nb <= HEAD_DIM
    row = lambda i: (i, 0)
    col = lambda i: (0, i)
    return pl.pallas_call(
        _qkv_body, grid=(s // tm,),
        in_specs=[pl.BlockSpec((tm, d), row)] + [_resident(w.shape) for w in (wq, wv, wkt, wvt)],
        out_specs=[pl.BlockSpec((tm, d), row),
                   pl.BlockSpec((d, tm), col),
                   pl.BlockSpec((d, tm), col),
                   pl.BlockSpec((N_HEADS, tm // BLOCK, LANES, BLOCK), lambda i: (0, i, 0, 0)),
                   pl.BlockSpec((tm, d), row),
                   pl.BlockSpec((1, d, tm // BLOCK), lambda i: (i, 0, 0))],
        out_shape=[jax.ShapeDtypeStruct((s, d), F32),
                   jax.ShapeDtypeStruct((d, s), F32),
                   jax.ShapeDtypeStruct((d, s), F32),
                   jax.ShapeDtypeStruct((N_HEADS, nb, LANES, BLOCK), BF16),
                   jax.ShapeDtypeStruct((s, d), BF16),
                   jax.ShapeDtypeStruct((s // tm, d, tm // BLOCK), F32)],
        compiler_params=_params("parallel"), name="qkv_prompt")(x, wq, wv, wkt, wvt)


def _split_bf16(x):
    hi = x.astype(BF16)
    return hi, (x - hi.astype(F32)).astype(BF16)


def _moba_body(thr_ref, tab_ref, q_ref, kt_ref, v_ref, km_ref, o_ref, bias_ref):
    hp = pl.program_id(0)
    it = pl.program_id(1)
    nb = km_ref.shape[1]
    rr = lax.broadcasted_iota(I32, (BLOCK, BLOCK), 0)
    cc = lax.broadcasted_iota(I32, (BLOCK, BLOCK), 1)

    @pl.when(it == 0)
    def _():
        for hh in range(2):
            head = hp * 2 + hh

            def fill(dl, carry):
                bias_ref[hh, dl] = _bias_of(rr - cc + dl * BLOCK, head, thr_ref, tab_ref)
                return carry
            lax.fori_loop(0, N_NEAR, fill, 0)
            bias_ref[hh, N_NEAR] = jnp.full((BLOCK, BLOCK), tab_ref[(N_BUCKETS - 1) * N_HEADS + head], F32)

    lane = lax.broadcasted_iota(I32, (Q_TILE, LANES), 1)
    own = 2 * it + lax.broadcasted_iota(I32, (Q_TILE, 1), 0) // BLOCK
    bcol = lax.broadcasted_iota(I32, (Q_TILE, nb), 1)
    causal = rr >= cc
    q = q_ref[...]
    km_hi, km_lo = _split_bf16(km_ref[...])

    def keys(hh, blk):
        return kt_ref[hh, blk]

    def values(blk):
        return v_ref[pl.ds(pl.multiple_of(blk * BLOCK, BLOCK), BLOCK), :]

    lhs, state, c_far = [], [], []
    for hh in range(2):
        in_q = (lane >= hh * HEAD_DIM) & (lane < (hh + 1) * HEAD_DIM)
        qh = jnp.where(in_q, q, 0.0)
        q_hi, q_lo = _split_bf16(qh)
        gate = _dot(q_hi, km_hi) + _dot(q_hi, km_lo) + _dot(q_lo, km_hi)
        i1, i2, i3 = _top3(jnp.where(bcol < own, gate, -jnp.inf), bcol, 1)
        e = lane - (1 - hh) * HEAD_DIM
        unsel = jnp.where((e == i1) | (e == i2) | (e == i3), 0.0, -1.0)
        qs = qh * QK_SCALE
        lhs.append(jnp.where(in_q, qs, unsel).astype(BF16))
        c_far.append(tab_ref[(N_BUCKETS - 1) * N_HEADS + hp * 2 + hh])

        lhs_own = jnp.where(in_q, qs, jnp.where(e == own, 0.0, unsel)).astype(BF16)
        ss = []
        for b in range(2):
            s = _dot(lhs_own, keys(hh, 2 * it + b))
            top = s[:BLOCK] + bias_ref[hh, 0]
            bot = s[BLOCK:] + bias_ref[hh, 1 - b]
            if b == 0:
                top = jnp.where(causal, top, NEG_INF)
            else:
                bot = jnp.where(causal, bot, NEG_INF)
            ss.append(jnp.concatenate([top, bot], axis=0))
        m = jnp.maximum(jnp.max(ss[0], axis=-1, keepdims=True), jnp.max(ss[1], axis=-1, keepdims=True))
        ps = [jnp.exp(s - m) for s in ss]
        l = jnp.sum(ps[0], axis=-1, keepdims=True) + jnp.sum(ps[1], axis=-1, keepdims=True)
        acc = _dot(ps[0].astype(BF16), values(2 * it)) + _dot(ps[1].astype(BF16), values(2 * it + 1))
        state += [m, l, acc]

    def step(jp, carry, near):
        out = []
        for hh in range(2):
            m, l, acc = carry[3 * hh:3 * hh + 3]
            ss = []
            for b in range(2):
                s = _dot(lhs[hh], keys(hh, 2 * jp + b))
                if near:
                    dl = 2 * (it - jp) - b
                    s = s + jnp.concatenate([bias_ref[hh, jnp.minimum(dl, N_NEAR)],
                                             bias_ref[hh, jnp.minimum(dl + 1, N_NEAR)]], axis=0)
                ss.append(s)
            rmax = jnp.max(jnp.maximum(ss[0], ss[1]), axis=-1, keepdims=True)
            if near:
                m_new = jnp.maximum(m, rmax)
                shift = m_new
            else:
                m_new = jnp.maximum(m, rmax + c_far[hh])
                shift = m_new - c_far[hh]
            a = jnp.exp(m - m_new)
            ps = [jnp.exp(s - shift) for s in ss]
            l = a * l + jnp.sum(ps[0] + ps[1], axis=-1, keepdims=True)
            acc = a * acc + _dot(ps[0].astype(BF16), values(2 * jp)) + _dot(ps[1].astype(BF16), values(2 * jp + 1))
            out += [m_new, l, acc]
        return tuple(out)

    n_far = jnp.maximum(it - (N_FAR_PAIR - 1), 0)
    state = lax.fori_loop(0, n_far, functools.partial(step, near=False), tuple(state))
    state = lax.fori_loop(n_far, it, functools.partial(step, near=True), state)
    o_ref[...] = jnp.where(lane < HEAD_DIM, state[2] / state[1], state[5] / state[4]).astype(BF16)


def moba_prompt(q, kta, vb, kmt, thr, tab):
    s, d = q.shape
    nb = s // BLOCK
    smem = pl.BlockSpec(memory_space=pltpu.SMEM)
    return pl.pallas_call(
        _moba_body, grid=(d // LANES, s // Q_TILE),
        in_specs=[smem, smem,
                  pl.BlockSpec((Q_TILE, LANES), lambda hp, i: (i, hp)),
                  pl.BlockSpec((2, nb, LANES, BLOCK), lambda hp, i: (hp, 0, 0, 0)),
                  pl.BlockSpec((s, LANES), lambda hp, i: (0, hp)),
                  pl.BlockSpec((LANES, nb), lambda hp, i: (hp, 0))],
        out_specs=pl.BlockSpec((Q_TILE, LANES), lambda hp, i: (i, hp)),
        out_shape=jax.ShapeDtypeStruct((s, d), BF16),
        scratch_shapes=[pltpu.VMEM((2, N_NEAR + 1, BLOCK, BLOCK), F32)],
        compiler_params=_params("arbitrary", "arbitrary"), name="moba_prompt")(thr, tab, q, kta, vb, kmt)


def _s5_body(x_ref, win_ref, bbr_ref, bbi_ref, cr_ref, ci_ref, ar_ref, ai_ref, d_ref, wglu_ref, bglu_ref,
             wout_ref, g_ref, b_ref, h0r_ref, h0i_ref, perm_ref, permt_ref, o_ref, hr_ref, hi_ref,
             u_ref, y_ref, sr_ref, si_ref, *chain_scratch, alpha, nseg, lseg, cw, chained):
    step = pl.program_id(0)
    slab = S5_SLAB * GROUP_P
    sw = S5_SLAB * STATE_N
    n_slab = win_ref.shape[1] // slab
    rows = lambda j: pl.ds(pl.multiple_of(j * nseg, 8), nseg)

    if chained:
        pwr_ref, pwi_ref, cyr_ref, cyi_ref = chain_scratch

        @pl.when(step == 0)
        def _():
            cyr_ref[...] = h0r_ref[...]
            cyi_ref[...] = h0i_ref[...]
            for c in range(ar_ref.shape[1] // sw):
                cs = slice(c * sw, (c + 1) * sw)
                ar, ai = ar_ref[:, cs], ai_ref[:, cs]

                def power(j, carry):
                    pr, pi = carry
                    pwr_ref[pl.ds(j, 1), cs] = pr
                    pwi_ref[pl.ds(j, 1), cs] = pi
                    return pr * ar - pi * ai, pr * ai + pi * ar
                lax.fori_loop(0, lseg, power, (ar, ai))

    x = x_ref[...]
    xb = _dot(perm_ref[...], x.astype(BF16)).astype(BF16)
    u_ref[...] = _dot(xb, win_ref[...])
    for s in range(n_slab):
        ub = u_ref[:, s * slab:(s + 1) * slab].astype(BF16)
        sr_ref[...] = _dot(ub, bbr_ref[s])
        si_ref[...] = _dot(ub, bbi_ref[s])
        for c in range(sw // cw):
            cs = slice(c * cw, (c + 1) * cw)
            gs = slice(s * sw + c * cw, s * sw + (c + 1) * cw)
            ar = jnp.broadcast_to(ar_ref[:, gs], (nseg, cw))
            ai = jnp.broadcast_to(ai_ref[:, gs], (nseg, cw))

            def scan(j, carry):
                hr, hi = carry
                nr = ar * hr - ai * hi + sr_ref[rows(j), cs]
                ni = ar * hi + ai * hr + si_ref[rows(j), cs]
                sr_ref[rows(j), cs] = nr
                si_ref[rows(j), cs] = ni
                return nr, ni

            if not chained:
                er, ei = lax.fori_loop(0, lseg, scan, (h0r_ref[:, gs], h0i_ref[:, gs]))
                hr_ref[:, gs] = er
                hi_ref[:, gs] = ei
            else:
                zero = jnp.zeros((nseg, cw), F32)
                er, ei = lax.fori_loop(0, lseg, scan, (zero, zero))
                plr, pli = pwr_ref[lseg - 1:lseg, gs], pwi_ref[lseg - 1:lseg, gs]
                cr, ci = cyr_ref[:, gs], cyi_ref[:, gs]
                hsr, hsi = [], []
                for r in range(nseg):
                    hsr.append(cr)
                    hsi.append(ci)
                    cr, ci = (er[r:r + 1] + plr * cr - pli * ci, ei[r:r + 1] + plr * ci + pli * cr)
                cyr_ref[:, gs] = cr
                cyi_ref[:, gs] = ci
                hsr = jnp.concatenate(hsr, axis=0)
                hsi = jnp.concatenate(hsi, axis=0)

                def fix(j, carry):
                    pr = jnp.broadcast_to(pwr_ref[pl.ds(j, 1), gs], (nseg, cw))
                    pi = jnp.broadcast_to(pwi_ref[pl.ds(j, 1), gs], (nseg, cw))
                    sr_ref[rows(j), cs] = sr_ref[rows(j), cs] + (pr * hsr - pi * hsi)
                    si_ref[rows(j), cs] = si_ref[rows(j), cs] + (pr * hsi + pi * hsr)
                    return carry
                lax.fori_loop(0, lseg, fix, 0)
        ys = _dot(sr_ref[...].astype(BF16), cr_ref[s]) - _dot(si_ref[...].astype(BF16), ci_ref[s])
        us = slice(s * slab, (s + 1) * slab)
        y_ref[:, us] = ys + d_ref[:, us] * u_ref[:, us]
    if chained:
        hr_ref[...] = cyr_ref[...]
        hi_ref[...] = cyi_ref[...]
    y = y_ref[...]
    z = jax.nn.gelu(y) * jax.nn.sigmoid(_dot(y.astype(BF16), wglu_ref[...]) + bglu_ref[...])
    zb = _dot(permt_ref[...], z.astype(BF16)).astype(BF16)
    m = _dot(zb, wout_ref[...])
    o_ref[...] = _layer_norm(alpha * x + m, g_ref[...], b_ref[...])


def s5_layer(x, h0r, h0i, w, g, b, alpha, *, nseg, lseg, cw, chained):
    m, d = x.shape
    tb = nseg * lseg
    n_state = h0r.shape[1]
    sw = S5_SLAB * STATE_N
    row = lambda i: (i, 0)
    hs = h0r.shape
    nat = jnp.arange(tb, dtype=I32)
    perm = ((nat[:, None] % nseg) * lseg + nat[:, None] // nseg == nat[None, :]).astype(BF16)
    scratch = [pltpu.VMEM((tb, d), F32), pltpu.VMEM((tb, d), F32), pltpu.VMEM((tb, sw), F32), pltpu.VMEM((tb, sw), F32)]
    if chained:
        scratch += [pltpu.VMEM((lseg, n_state), F32), pltpu.VMEM((lseg, n_state), F32),
                    pltpu.VMEM((1, n_state), F32), pltpu.VMEM((1, n_state), F32)]
    else:
        assert m == tb
    weights = [w["w_in"], w["bb_re"], w["bb_im"], w["c_re"], w["c_im"], w["a_re"], w["a_im"], w["d"],
               w["w_glu"], w["b_glu"], w["w_out"], g, b, h0r, h0i, perm, perm.T]
    return pl.pallas_call(
        functools.partial(_s5_body, alpha=alpha, nseg=nseg, lseg=lseg, cw=cw, chained=chained),
        grid=(m // tb,),
        in_specs=[pl.BlockSpec((tb, d), row)] + [_resident(a.shape) for a in weights],
        out_specs=[pl.BlockSpec((tb, d), row), pl.BlockSpec(hs, lambda i: (0, 0)), pl.BlockSpec(hs, lambda i: (0, 0))],
        out_shape=[jax.ShapeDtypeStruct((m, d), F32), jax.ShapeDtypeStruct(hs, F32), jax.ShapeDtypeStruct(hs, F32)],
        scratch_shapes=scratch, compiler_params=_params("arbitrary"), name="s5_layer")(x, *weights)


def s5_weights(w_in, lam_re, lam_im, log_dt, b_re, b_im, c_re, c_im, d, w_glu, b_glu, w_out):
    dt = jnp.exp(log_dt)[:, None]
    mag = jnp.exp(lam_re * dt)
    ang = lam_im * dt
    ab_re, ab_im = mag * jnp.cos(ang), mag * jnp.sin(ang)
    den = lam_re * lam_re + lam_im * lam_im
    f_re = ((ab_re - 1.0) * lam_re + ab_im * lam_im) / den
    f_im = (ab_im * lam_re - (ab_re - 1.0) * lam_im) / den
    bb_re = f_re[..., None] * b_re - f_im[..., None] * b_im
    bb_im = f_re[..., None] * b_im + f_im[..., None] * b_re
    n_g = lam_re.shape[0]
    eye = jnp.eye(S5_SLAB, dtype=F32)

    def in_slabs(bb):
        t = jnp.transpose(bb, (0, 2, 1)).reshape(n_g // S5_SLAB, S5_SLAB, GROUP_P, STATE_N)
        return jnp.einsum("sgpn,gh->sgphn", t, eye).reshape(n_g // S5_SLAB, S5_SLAB * GROUP_P, S5_SLAB * STATE_N).astype(BF16)

    def out_slabs(c):
        t = jnp.transpose(c, (0, 2, 1)).reshape(n_g // S5_SLAB, S5_SLAB, STATE_N, GROUP_P)
        return jnp.einsum("sgnp,gh->sgnhp", t, eye).reshape(n_g // S5_SLAB, S5_SLAB * STATE_N, S5_SLAB * GROUP_P).astype(BF16)

    return dict(w_in=w_in.astype(BF16), bb_re=in_slabs(bb_re), bb_im=in_slabs(bb_im), c_re=out_slabs(c_re),
                c_im=out_slabs(c_im), a_re=ab_re.reshape(1, -1), a_im=ab_im.reshape(1, -1), d=d.reshape(1, -1),
                w_glu=w_glu.astype(BF16), b_glu=b_glu.reshape(1, -1), w_out=w_out.astype(BF16))


def _sgate_body(pt_ref, q_ref, *rest, n_pg, nb):
    pages, idx_ref, km_ref = rest[:n_pg], rest[n_pg], rest[n_pg + 1]
    pc = pl.program_id(1)
    ppb = BLOCK // PAGE_SIZE
    lane = lax.broadcasted_iota(I32, km_ref.shape, 2)

    @pl.when(pc == 0)
    def _():
        km_ref[...] = jnp.zeros(km_ref.shape, F32)

    km = km_ref[...]
    for u in range(n_pg // ppb):
        tot = functools.reduce(jnp.add, [pages[u * ppb + t][0, 0] for t in range(ppb)])
        mean = jnp.sum(tot, axis=-1, keepdims=True) * (1.0 / BLOCK)
        km = jnp.where(lane == pc * (n_pg // ppb) + u, mean, km)
    km_ref[...] = km

    @pl.when(pc == pl.num_programs(1) - 1)
    def _():
        km = km_ref[...]
        bl = lax.broadcasted_iota(I32, (km.shape[0], km.shape[2]), 1)
        for qi in range(q_ref.shape[3]):
            gate = jnp.sum(km * q_ref[0, :, :, qi:qi + 1], axis=1)
            for t, am in enumerate(_top3(jnp.where(bl < nb, gate, -jnp.inf), bl, 1)):
                idx_ref[0, qi, :, t:t + 1] = am


def sample_gate(qt, cache_kt, page_table, li):
    db, nh, dh, nq = qt.shape
    n_pages = page_table.shape[1]
    nb = n_pages * PAGE_SIZE // BLOCK
    assert nb <= LANES
    n_pg = 8
    page = lambda t: pl.BlockSpec((1, 1, nh, dh, PAGE_SIZE),
                                  lambda b, pc, pt: (li, pt[b * n_pages + pc * n_pg + t], 0, 0, 0))
    return pl.pallas_call(
        functools.partial(_sgate_body, n_pg=n_pg, nb=nb),
        grid_spec=pltpu.PrefetchScalarGridSpec(
            num_scalar_prefetch=1, grid=(db, n_pages // n_pg),
            in_specs=[pl.BlockSpec((1, nh, dh, nq), lambda b, pc, pt: (b, 0, 0, 0))] + [page(t) for t in range(n_pg)],
            out_specs=pl.BlockSpec((1, nq, nh, TOPK), lambda b, pc, pt: (b, 0, 0, 0)),
            scratch_shapes=[pltpu.VMEM((nh, dh, LANES), F32)]),
        out_shape=jax.ShapeDtypeStruct((db, nq, nh, TOPK), I32),
        compiler_params=_params("arbitrary", "arbitrary"), name="sample_gate")(
            page_table.reshape(-1), qt, *([cache_kt] * n_pg))


def _sattn_body(pt_ref, idx_ref, thr_ref, tab_ref, q_ref, kn_ref, vn_ref, *rest, nq, past_len):
    n_pg = nq * TOPK * (BLOCK // PAGE_SIZE)
    kpg, vpg, o_ref = rest[:n_pg], rest[n_pg:2 * n_pg], rest[2 * n_pg]
    b = pl.program_id(0)
    h = pl.program_id(1)
    ppb = BLOCK // PAGE_SIZE
    far = thr_ref[N_BUCKETS - 1]
    last = tab_ref[(N_BUCKETS - 1) * N_HEADS + h]
    kn, vn = kn_ref[0, 0], vn_ref[0, 0]
    off = lax.broadcasted_iota(I32, (1, BLOCK), 1)
    d_new = lax.broadcasted_iota(I32, (nq, nq), 0) - lax.broadcasted_iota(I32, (nq, nq), 1)
    bias_new = _bias_of(d_new, h, thr_ref, tab_ref, unroll=True)
    for qi in range(nq):
        qcol = q_ref[0, 0, :, qi:qi + 1] * QK_SCALE
        scores = []
        for t in range(TOPK):
            blk = idx_ref[((b * nq + qi) * N_HEADS + h) * TOPK + t]
            dist = (past_len + qi) - (blk * BLOCK + off)
            bias = lax.cond((past_len + qi) - (blk * BLOCK + BLOCK - 1) >= far,
                            lambda: jnp.full((1, BLOCK), last, F32),
                            lambda dist=dist: _bias_of(dist, h, thr_ref, tab_ref, unroll=True))
            for pp in range(ppb):
                kt = kpg[(qi * TOPK + t) * ppb + pp][0, 0, 0]
                scores.append(jnp.sum(kt * qcol, axis=0, keepdims=True)
                              + bias[:, pp * PAGE_SIZE:(pp + 1) * PAGE_SIZE])
        s_own = jnp.sum(kn * qcol, axis=0, keepdims=True) + bias_new[qi:qi + 1, :]
        scores.append(jnp.where(d_new[qi:qi + 1, :] >= 0, s_own, NEG_INF))
        m = functools.reduce(jnp.maximum, [jnp.max(s, axis=1, keepdims=True) for s in scores])
        ps = [jnp.exp(s - m) for s in scores]
        l = functools.reduce(jnp.add, [jnp.sum(p, axis=1, keepdims=True) for p in ps])
        acc = functools.reduce(jnp.add, [vpg[qi * TOPK * ppb + j][0, 0, 0] * (ps[j] / l) for j in range(TOPK * ppb)])
        o = jnp.sum(acc, axis=1, keepdims=True) + jnp.sum(vn * (ps[-1] / l), axis=1, keepdims=True)
        o_ref[0, 0, :, qi:qi + 1] = o


def sample_attn(qt, knt, vnt, cache_kt, cache_vt, page_table, idx, thr, tab, li, past_len):
    db, nh, dh, nq = qt.shape
    n_pages = page_table.shape[1]
    ppb = BLOCK // PAGE_SIZE

    def page(qi, t, pp):
        def index(b, h, pt, ix, th):
            blk = ix[((b * nq + qi) * nh + h) * TOPK + t]
            return (li, pt[b * n_pages + blk * ppb + pp], h, 0, 0)
        return pl.BlockSpec((1, 1, 1, dh, PAGE_SIZE), index)

    pages = [page(qi, t, pp) for qi in range(nq) for t in range(TOPK) for pp in range(ppb)]
    tok = pl.BlockSpec((1, 1, dh, nq), lambda b, h, pt, ix, th: (b, h, 0, 0))
    return pl.pallas_call(
        functools.partial(_sattn_body, nq=nq, past_len=past_len),
        grid_spec=pltpu.PrefetchScalarGridSpec(
            num_scalar_prefetch=3, grid=(db, nh),
            in_specs=[pl.BlockSpec(memory_space=pltpu.SMEM), tok, tok, tok] + pages + pages,
            out_specs=tok),
        out_shape=jax.ShapeDtypeStruct((db, nh, dh, nq), F32),
        compiler_params=_params("arbitrary", "arbitrary"), name="sample_attn")(
            page_table.reshape(-1), idx.reshape(-1), thr, tab, qt, knt, vnt,
            *([cache_kt] * len(pages)), *([cache_vt] * len(pages)))


def _bucket_thresholds():
    n = jnp.arange(MAX_DISTANCE + 1, dtype=I32)
    nf = jnp.maximum(n, 1).astype(F32)
    log_b = MAX_EXACT + (jnp.log(nf / MAX_EXACT) / math.log(MAX_DISTANCE / MAX_EXACT)
                         * (N_BUCKETS - MAX_EXACT)).astype(I32)
    bucket = jnp.where(n < MAX_EXACT, n, jnp.minimum(log_b, N_BUCKETS - 1))
    return jnp.sum(bucket[None, :] < jnp.arange(N_BUCKETS, dtype=I32)[:, None], axis=1).astype(I32)


def _heads(t, b, s):
    return t.reshape(b, s, N_HEADS, HEAD_DIM).transpose(0, 2, 1, 3)


def kernel(x_prompt, x_sample, cache_k, cache_v, state_s5_re, state_s5_im, page_table, p_prompt, p_sample, ln_g, ln_b, ffn_w1, ffn_w3, ffn_w2, ple_w_proj, ple_w_gate, rel_bias_table, attn_w_qkv, attn_w_o, s5_w_in, s5_lambda_re, s5_lambda_im, s5_log_dt, s5_b_re, s5_b_im, s5_c_re, s5_c_im, s5_d, s5_w_glu, s5_b_glu, s5_w_out):
    depth = ln_g.shape[0]
    alpha = (2 * depth) ** 0.25
    bsz, seq, d = x_prompt.shape
    db, nq, _ = x_sample.shape
    assert bsz == 1 and seq % (2 * BLOCK) == 0 and d == N_HEADS * HEAD_DIM
    past_len = page_table.shape[1] * PAGE_SIZE
    assert past_len % BLOCK == 0
    xp = x_prompt.reshape(seq, d)
    xs = x_sample.reshape(db * nq, d)
    thr = _bucket_thresholds()
    tab = rel_bias_table.reshape(-1)
    vec = lambda a: a.reshape(1, -1)
    outs = {k: [] for k in ("kp", "vp", "ks", "vs", "spr", "spi", "ssr", "ssi")}
    for i in range(depth):
        li = i // 2
        w1, w3, w2 = ffn_w1[i].astype(BF16), ffn_w3[i].astype(BF16), ffn_w2[i].astype(BF16)
        g, b = ln_g[i], ln_b[i]
        xp = ffn_ln(xp, w1[0], w3[0], w2[0], vec(g[0]), vec(b[0]), alpha)
        xs = ffn_ln(xs, w1[0], w3[0], w2[0], vec(g[0]), vec(b[0]), alpha)
        if i % 2 == 0:
            wqkv = attn_w_qkv[li].astype(BF16)
            wq, wk, wv = wqkv[:, :d], wqkv[:, d:2 * d], wqkv[:, 2 * d:]
            wo = attn_w_o[li].astype(BF16)
            q, kto, vto, kta, vb, km = qkv_prompt(xp, wq, wv, wk.T, wv.T)
            kmt = jnp.transpose(km, (1, 0, 2)).reshape(d, seq // BLOCK)
            o = moba_prompt(q, kta, vb, kmt, thr, tab)
            xp = lin_ln(xp, o, wo, vec(g[1]), vec(b[1]), alpha)
            outs["kp"].append(jnp.swapaxes(kto.reshape(1, N_HEADS, HEAD_DIM, seq), 2, 3))
            outs["vp"].append(jnp.swapaxes(vto.reshape(1, N_HEADS, HEAD_DIM, seq), 2, 3))

            qkv_s = linear(xs, wqkv)
            qs, ks, vs = (_heads(t, db, nq) for t in jnp.split(qkv_s, 3, axis=-1))
            ckt, cvt = jnp.swapaxes(cache_k, 3, 4), jnp.swapaxes(cache_v, 3, 4)
            qt, knt, vnt = (jnp.swapaxes(t, 2, 3) for t in (qs, ks, vs))
            idx = sample_gate(qt, ckt, page_table, li)
            o_s = sample_attn(qt, knt, vnt, ckt, cvt, page_table, idx, thr, tab, li, past_len)
            xs = lin_ln(xs, o_s.transpose(0, 3, 1, 2).reshape(db * nq, d), wo, vec(g[1]), vec(b[1]), alpha)
            outs["ks"].append(ks)
            outs["vs"].append(vs)
        else:
            w = s5_weights(s5_w_in[li], s5_lambda_re[li], s5_lambda_im[li], s5_log_dt[li], s5_b_re[li], s5_b_im[li],
                           s5_c_re[li], s5_c_im[li], s5_d[li], s5_w_glu[li], s5_b_glu[li], s5_w_out[li])
            n_state = w["a_re"].shape[1]
            zero = jnp.zeros((1, n_state), F32)
            xp, hr, hi = s5_layer(xp, zero, zero, w, vec(g[1]), vec(b[1]), alpha, nseg=8, lseg=64, cw=1024, chained=True)
            outs["spr"].append(hr.reshape(1, -1, STATE_N))
            outs["spi"].append(hi.reshape(1, -1, STATE_N))
            xs, hr, hi = s5_layer(xs, state_s5_re[li].reshape(db, n_state), state_s5_im[li].reshape(db, n_state), w,
                                  vec(g[1]), vec(b[1]), alpha, nseg=db, lseg=nq, cw=256, chained=False)
            outs["ssr"].append(hr.reshape(db, -1, STATE_N))
            outs["ssi"].append(hi.reshape(db, -1, STATE_N))
        ple_w = (ple_w_gate[i].astype(BF16), ple_w_proj[i].astype(BF16))
        xp = ffn_ln(xp, w1[1], w3[1], w2[1], vec(g[2]), vec(b[2]), alpha, ple=(p_prompt[i].reshape(seq, -1),) + ple_w)
        xs = ffn_ln(xs, w1[1], w3[1], w2[1], vec(g[2]), vec(b[2]), alpha, ple=(p_sample[i].reshape(db * nq, -1),) + ple_w)
    st = lambda k: jnp.stack(outs[k])
    return (xp.reshape(bsz, seq, d), xs.reshape(db, nq, d), st("kp"), st("vp"), st("ks"), st("vs"),
            st("spr"), st("spi"), st("ssr"), st("ssi"))
```

```python
import functools
import math

import jax
import jax.numpy as jnp
from jax import lax
from jax.experimental import pallas as pl
from jax.experimental.pallas import tpu as pltpu

F32 = jnp.float32
BF16 = jnp.bfloat16
I32 = jnp.int32

N_HEADS = 16
HEAD_DIM = 64
BLOCK = 256
TOPK = 3
PAGE_SIZE = 128
N_BUCKETS = 32
MAX_EXACT = N_BUCKETS // 2
MAX_DISTANCE = 4096
GROUP_P = 16
STATE_N = 64
LN_EPS = 1e-5
NEG_INF = -1e30
MASK_BIG = 1e30
QK_SCALE = HEAD_DIM ** -0.5

LANES = 128
VMEM_LIMIT = 56 * 1024 * 1024

N_NEAR = MAX_DISTANCE // BLOCK + 1
Q_TILE = 2 * BLOCK
N_FAR_PAIR = (N_NEAR + 2) // 2
S5_SLAB = 16


def _params(*sem, flags=None):
    return pltpu.CompilerParams(dimension_semantics=sem, vmem_limit_bytes=VMEM_LIMIT, flags=flags)


def _resident(shape):
    nd = len(shape)
    return pl.BlockSpec(shape, lambda *_: (0,) * nd, pipeline_mode=pl.Buffered(1))


def _dot(a, b):
    return jnp.dot(a, b, preferred_element_type=F32)


def _layer_norm(y, g, b):
    mu = jnp.mean(y, axis=-1, keepdims=True)
    yc = y - mu
    var = jnp.mean(yc * yc, axis=-1, keepdims=True)
    return yc * lax.rsqrt(var + LN_EPS) * g + b


def _top3(work, cols, axis):
    out = []
    big = jnp.int32(1 << 30)
    for _ in range(TOPK):
        mx = jnp.max(work, axis=axis, keepdims=True)
        am = jnp.min(jnp.where(work == mx, cols, big), axis=axis, keepdims=True)
        out.append(jnp.where(mx > -jnp.inf, am, -1))
        work = jnp.where(cols == am, -jnp.inf, work)
    return out


def _bias_of(dist, head, thr_ref, tab_ref, unroll=False):
    def step(b, acc):
        return jnp.where(dist >= thr_ref[b], tab_ref[b * N_HEADS + head], acc)
    acc = jnp.full(dist.shape, tab_ref[head], F32)
    if unroll:
        for b in range(1, N_BUCKETS):
            acc = step(b, acc)
        return acc
    return lax.fori_loop(1, N_BUCKETS, step, acc)


def _ffn_body(x_ref, w1_ref, w3_ref, w2_ref, g_ref, b_ref, *rest, alpha, tf, ple):
    if ple:
        p_ref, wg_ref, wp_ref, o_ref = rest
    else:
        (o_ref,) = rest
    x = x_ref[...]
    xb = x.astype(BF16)
    acc = jnp.zeros(x.shape, F32)
    for j in range(w1_ref.shape[1] // tf):
        sl = slice(j * tf, (j + 1) * tf)
        h = jax.nn.silu(_dot(xb, w1_ref[:, sl])) * _dot(xb, w3_ref[:, sl])
        acc = acc + _dot(h.astype(BF16), w2_ref[sl, :])
    y = _layer_norm(alpha * x + 0.5 * acc, g_ref[...], b_ref[...])
    if ple:
        gate = jax.nn.sigmoid(_dot(y.astype(BF16), wg_ref[...]))
        y = y + gate * _dot(p_ref[...].astype(BF16), wp_ref[...])
    o_ref[...] = y


def ffn_ln(x, w1, w3, w2, g, b, alpha, ple=None):
    m, d = x.shape
    tm = min(512, m)
    row = lambda i: (i, 0)
    in_specs = [pl.BlockSpec((tm, d), row), _resident(w1.shape), _resident(w3.shape), _resident(w2.shape),
                _resident(g.shape), _resident(b.shape)]
    args = [x, w1, w3, w2, g, b]
    if ple is not None:
        p, wg, wp = ple
        in_specs += [pl.BlockSpec((tm, p.shape[1]), row), _resident(wg.shape), _resident(wp.shape)]
        args += [p, wg, wp]
    return pl.pallas_call(
        functools.partial(_ffn_body, alpha=alpha, tf=2 * LANES, ple=ple is not None),
        grid=(m // tm,), in_specs=in_specs, out_specs=pl.BlockSpec((tm, d), row),
        out_shape=jax.ShapeDtypeStruct((m, d), F32), compiler_params=_params("parallel"),
        name="ffn_ln")(*args)


def _lin_ln_body(x_ref, a_ref, w_ref, g_ref, b_ref, o_ref, *, alpha):
    m = _dot(a_ref[...].astype(BF16), w_ref[...])
    o_ref[...] = _layer_norm(alpha * x_ref[...] + m, g_ref[...], b_ref[...])


def lin_ln(x, a, w, g, b, alpha):
    m, d = x.shape
    tm = min(512, m)
    row = lambda i: (i, 0)
    return pl.pallas_call(
        functools.partial(_lin_ln_body, alpha=alpha),
        grid=(m // tm,),
        in_specs=[pl.BlockSpec((tm, d), row), pl.BlockSpec((tm, a.shape[1]), row), _resident(w.shape),
                  _resident(g.shape), _resident(b.shape)],
        out_specs=pl.BlockSpec((tm, d), row), out_shape=jax.ShapeDtypeStruct((m, d), F32),
        compiler_params=_params("parallel"), name="lin_ln")(x, a, w, g, b)


def _linear_body(x_ref, w_ref, o_ref):
    o_ref[...] = _dot(x_ref[...].astype(BF16), w_ref[...])


def linear(x, w):
    m = x.shape[0]
    return pl.pallas_call(
        _linear_body, grid=(1,), in_specs=[_resident(x.shape), _resident(w.shape)],
        out_specs=pl.BlockSpec((m, w.shape[1]), lambda i: (0, 0)),
        out_shape=jax.ShapeDtypeStruct((m, w.shape[1]), F32), compiler_params=_params("arbitrary"),
        name="linear")(x, w)


def _qkv_body(x_ref, wq_ref, wv_ref, wkt_ref, wvt_ref, q_ref, kto_ref, vto_ref, kta_ref, vb_ref, km_ref):
    xb = x_ref[...].astype(BF16)
    tm, d = xb.shape
    nblk = tm // BLOCK
    q_ref[...] = _dot(xb, wq_ref[...])
    vb_ref[...] = _dot(xb, wv_ref[...]).astype(BF16)
    nt = lambda w_ref: lax.dot_general(w_ref[...], xb, (((1,), (1,)), ((), ())), preferred_element_type=F32)
    kt = nt(wkt_ref)
    kto_ref[...] = kt
    vto_ref[...] = nt(wvt_ref)
    ktb = kt.astype(BF16)
    row = lax.broadcasted_iota(I32, (LANES, BLOCK), 0)
    for bk in range(nblk):
        n = pl.program_id(0) * nblk + bk
        cols = slice(bk * BLOCK, (bk + 1) * BLOCK)
        km_ref[0, :, bk:bk + 1] = jnp.sum(kt[:, cols], axis=1, keepdims=True) * (1.0 / BLOCK)
        mask_rows = jnp.where(row % HEAD_DIM == n, MASK_BIG, 0.0).astype(BF16)
        for p in range(d // LANES):
            pair = ktb[p * LANES:(p + 1) * LANES, cols]
            kta_ref[2 * p, bk] = jnp.where(row < HEAD_DIM, pair, mask_rows)
            kta_ref[2 * p + 1, bk] = jnp.where(row >= HEAD_DIM, pair, mask_rows)


def qkv_prompt(x, wq, wv, wkt, wvt):
    s, d = x.shape
    tm = 2 * BLOCK
    nb = s // BLOCK
    assert nb <= HEAD_DIM
    row = lambda i: (i, 0)
    col = lambda i: (0, i)
    return pl.pallas_call(
        _qkv_body, grid=(s // tm,),
        in_specs=[pl.BlockSpec((tm, d), row)] + [_resident(w.shape) for w in (wq, wv, wkt, wvt)],
        out_specs=[pl.BlockSpec((tm, d), row),
                   pl.BlockSpec((d, tm), col),
                   pl.BlockSpec((d, tm), col),
                   pl.BlockSpec((N_HEADS, tm // BLOCK, LANES, BLOCK), lambda i: (0, i, 0, 0)),
                   pl.BlockSpec((tm, d), row),
                   pl.BlockSpec((1, d, tm // BLOCK), lambda i: (i, 0, 0))],
        out_shape=[jax.ShapeDtypeStruct((s, d), F32),
                   jax.ShapeDtypeStruct((d, s), F32),
                   jax.ShapeDtypeStruct((d, s), F32),
                   jax.ShapeDtypeStruct((N_HEADS, nb, LANES, BLOCK), BF16),
                   jax.ShapeDtypeStruct((s, d), BF16),
                   jax.ShapeDtypeStruct((s // tm, d, tm // BLOCK), F32)],
        compiler_params=_params("parallel"), name="qkv_prompt")(x, wq, wv, wkt, wvt)


def _split_bf16(x):
    hi = x.astype(BF16)
    return hi, (x - hi.astype(F32)).astype(BF16)


def _moba_body(thr_ref, tab_ref, q_ref, kt_ref, v_ref, km_ref, o_ref, bias_ref, s_ref, p_ref):
    hp = pl.program_id(0)
    it = pl.program_id(1)
    nb = km_ref.shape[1]
    rr = lax.broadcasted_iota(I32, (BLOCK, BLOCK), 0)
    cc = lax.broadcasted_iota(I32, (BLOCK, BLOCK), 1)

    @pl.when(it == 0)
    def _():
        for hh in range(2):
            head = hp * 2 + hh

            def fill(dl, carry):
                bias_ref[hh, dl] = _bias_of(rr - cc + dl * BLOCK, head, thr_ref, tab_ref)
                return carry
            lax.fori_loop(0, N_NEAR, fill, 0)
            bias_ref[hh, N_NEAR] = jnp.full((BLOCK, BLOCK), tab_ref[(N_BUCKETS - 1) * N_HEADS + head], F32)

    lane = lax.broadcasted_iota(I32, (Q_TILE, LANES), 1)
    own = 2 * it + lax.broadcasted_iota(I32, (Q_TILE, 1), 0) // BLOCK
    bcol = lax.broadcasted_iota(I32, (Q_TILE, nb), 1)
    causal = rr >= cc
    q = q_ref[...]
    km_hi, km_lo = _split_bf16(km_ref[...])

    def keys(hh, blk):
        return kt_ref[hh, blk]

    vlane = lax.broadcasted_iota(I32, (Q_TILE, LANES), 1)

    def attend(hh, p, pair):
        v = v_ref[pl.ds(pl.multiple_of(pair * Q_TILE, Q_TILE), Q_TILE), :]
        mine = (vlane >= hh * HEAD_DIM) & (vlane < (hh + 1) * HEAD_DIM)
        return _dot(p, jnp.where(mine, v, jnp.ones_like(v)))

    lhs, state, c_far = [], [], []
    for hh in range(2):
        in_q = (lane >= hh * HEAD_DIM) & (lane < (hh + 1) * HEAD_DIM)
        qh = jnp.where(in_q, q, 0.0)
        q_hi, q_lo = _split_bf16(qh)
        gate = _dot(q_hi, km_hi) + _dot(q_hi, km_lo) + _dot(q_lo, km_hi)
        i1, i2, i3 = _top3(jnp.where(bcol < own, gate, -jnp.inf), bcol, 1)
        e = lane - (1 - hh) * HEAD_DIM
        unsel = jnp.where((e == i1) | (e == i2) | (e == i3), 0.0, -1.0)
        qs = qh * QK_SCALE
        lhs.append(jnp.where(in_q, qs, unsel).astype(BF16))
        c_far.append(tab_ref[(N_BUCKETS - 1) * N_HEADS + hp * 2 + hh])

        lhs_own = jnp.where(in_q, qs, jnp.where(e == own, 0.0, unsel)).astype(BF16)
        ss = []
        for b in range(2):
            s = _dot(lhs_own, keys(hh, 2 * it + b))
            top = s[:BLOCK] + bias_ref[hh, 0]
            bot = s[BLOCK:] + bias_ref[hh, 1 - b]
            if b == 0:
                top = jnp.where(causal, top, NEG_INF)
            else:
                bot = jnp.where(causal, bot, NEG_INF)
            ss.append(jnp.concatenate([top, bot], axis=0))
        m = jnp.max(jnp.maximum(ss[0], ss[1]), axis=-1, keepdims=True)
        p = jnp.concatenate([jnp.exp(s - m).astype(BF16) for s in ss], axis=1)
        state += [m, attend(hh, p, it)]

    def step(jp, carry, near):
        for hh in range(2):
            for b in range(2):
                s_ref[hh, :, b * BLOCK:(b + 1) * BLOCK] = _dot(lhs[hh], keys(hh, 2 * jp + b))
        out = []
        for hh in range(2):
            m, acc = carry[2 * hh:2 * hh + 2]

            def scores(halves):
                s = (jnp.concatenate([s_ref[hh, :, :BLOCK], s_ref[hh, :, BLOCK:]], axis=1) if halves
                     else s_ref[hh])
                if near:
                    dl = 2 * (it - jp)
                    tile = lambda a, b: bias_ref[hh, jnp.minimum(dl + a - b, N_NEAR)]
                    s = s + jnp.concatenate([jnp.concatenate([tile(0, 0), tile(0, 1)], axis=1),
                                             jnp.concatenate([tile(1, 0), tile(1, 1)], axis=1)], axis=0)
                return s

            rmax = jnp.max(scores(False), axis=-1, keepdims=True)
            if near:
                m_new = jnp.maximum(m, rmax)
                shift = m_new
            else:
                m_new = jnp.maximum(m, rmax + c_far[hh])
                shift = m_new - c_far[hh]
            p_ref[hh] = jnp.exp(scores(True) - shift).astype(BF16)
            out += [m_new, jnp.exp(m - m_new) * acc + attend(hh, p_ref[hh], jp)]
        return tuple(out)

    n_far = jnp.maximum(it - (N_FAR_PAIR - 1), 0)
    state = lax.fori_loop(0, n_far, functools.partial(step, near=False), tuple(state))
    state = lax.fori_loop(n_far, it, functools.partial(step, near=True), state)
    l0 = state[1][:, HEAD_DIM:HEAD_DIM + 1]
    l1 = state[3][:, 0:1]
    o_ref[...] = jnp.where(lane < HEAD_DIM, state[1] / l0, state[3] / l1).astype(BF16)


def moba_prompt(q, kta, vb, kmt, thr, tab):
    s, d = q.shape
    nb = s // BLOCK
    smem = pl.BlockSpec(memory_space=pltpu.SMEM)
    return pl.pallas_call(
        _moba_body, grid=(d // LANES, s // Q_TILE),
        in_specs=[smem, smem,
                  pl.BlockSpec((Q_TILE, LANES), lambda hp, i: (i, hp)),
                  pl.BlockSpec((2, nb, LANES, BLOCK), lambda hp, i: (hp, 0, 0, 0)),
                  pl.BlockSpec((s, LANES), lambda hp, i: (0, hp)),
                  pl.BlockSpec((LANES, nb), lambda hp, i: (hp, 0))],
        out_specs=pl.BlockSpec((Q_TILE, LANES), lambda hp, i: (i, hp)),
        out_shape=jax.ShapeDtypeStruct((s, d), BF16),
        scratch_shapes=[pltpu.VMEM((2, N_NEAR + 1, BLOCK, BLOCK), F32), pltpu.VMEM((2, Q_TILE, Q_TILE), F32),
                        pltpu.VMEM((2, Q_TILE, Q_TILE), BF16)],
        compiler_params=_params("arbitrary", "arbitrary"),
        name="moba_prompt")(thr, tab, q, kta, vb, kmt)


def _s5_body(x_ref, win_ref, bbr_ref, bbi_ref, cr_ref, ci_ref, ar_ref, ai_ref, d_ref, wglu_ref, bglu_ref,
             wout_ref, g_ref, b_ref, h0r_ref, h0i_ref, perm_ref, permt_ref, o_ref, hr_ref, hi_ref,
             u_ref, y_ref, sr_ref, si_ref, *chain_scratch, alpha, nseg, lseg, cw, chained):
    step = pl.program_id(0)
    slab = S5_SLAB * GROUP_P
    sw = S5_SLAB * STATE_N
    n_slab = win_ref.shape[1] // slab
    rows = lambda j: pl.ds(pl.multiple_of(j * nseg, 8), nseg)

    if chained:
        pwr_ref, pwi_ref, cyr_ref, cyi_ref = chain_scratch

        @pl.when(step == 0)
        def _():
            cyr_ref[...] = h0r_ref[...]
            cyi_ref[...] = h0i_ref[...]
            for c in range(ar_ref.shape[1] // sw):
                cs = slice(c * sw, (c + 1) * sw)
                ar, ai = ar_ref[:, cs], ai_ref[:, cs]

                def power(j, carry):
                    pr, pi = carry
                    pwr_ref[pl.ds(j, 1), cs] = pr
                    pwi_ref[pl.ds(j, 1), cs] = pi
                    return pr * ar - pi * ai, pr * ai + pi * ar
                lax.fori_loop(0, lseg, power, (ar, ai))

    x = x_ref[...]
    xb = _dot(perm_ref[...], x.astype(BF16)).astype(BF16)
    u_ref[...] = _dot(xb, win_ref[...])
    for s in range(n_slab):
        ub = u_ref[:, s * slab:(s + 1) * slab].astype(BF16)
        sr_ref[...] = _dot(ub, bbr_ref[s])
        si_ref[...] = _dot(ub, bbi_ref[s])
        for c in range(sw // cw):
            cs = slice(c * cw, (c + 1) * cw)
            gs = slice(s * sw + c * cw, s * sw + (c + 1) * cw)
            ar = jnp.broadcast_to(ar_ref[:, gs], (nseg, cw))
            ai = jnp.broadcast_to(ai_ref[:, gs], (nseg, cw))

            def scan(j, carry):
                hr, hi = carry
                nr = ar * hr - ai * hi + sr_ref[rows(j), cs]
                ni = ar * hi + ai * hr + si_ref[rows(j), cs]
                sr_ref[rows(j), cs] = nr
                si_ref[rows(j), cs] = ni
                return nr, ni

            if not chained:
                er, ei = lax.fori_loop(0, lseg, scan, (h0r_ref[:, gs], h0i_ref[:, gs]))
                hr_ref[:, gs] = er
                hi_ref[:, gs] = ei
            else:
                zero = jnp.zeros((nseg, cw), F32)
                er, ei = lax.fori_loop(0, lseg, scan, (zero, zero))
                plr, pli = pwr_ref[lseg - 1:lseg, gs], pwi_ref[lseg - 1:lseg, gs]
                cr, ci = cyr_ref[:, gs], cyi_ref[:, gs]
                hsr, hsi = [], []
                for r in range(nseg):
                    hsr.append(cr)
                    hsi.append(ci)
                    cr, ci = (er[r:r + 1] + plr * cr - pli * ci, ei[r:r + 1] + plr * ci + pli * cr)
                cyr_ref[:, gs] = cr
                cyi_ref[:, gs] = ci
                hsr = jnp.concatenate(hsr, axis=0)
                hsi = jnp.concatenate(hsi, axis=0)

                def fix(j, carry):
                    pr = jnp.broadcast_to(pwr_ref[pl.ds(j, 1), gs], (nseg, cw))
                    pi = jnp.broadcast_to(pwi_ref[pl.ds(j, 1), gs], (nseg, cw))
                    sr_ref[rows(j), cs] = sr_ref[rows(j), cs] + (pr * hsr - pi * hsi)
                    si_ref[rows(j), cs] = si_ref[rows(j), cs] + (pr * hsi + pi * hsr)
                    return carry
                lax.fori_loop(0, lseg, fix, 0)
        ys = _dot(sr_ref[...].astype(BF16), cr_ref[s]) - _dot(si_ref[...].astype(BF16), ci_ref[s])
        us = slice(s * slab, (s + 1) * slab)
        y_ref[:, us] = ys + d_ref[:, us] * u_ref[:, us]
    if chained:
        hr_ref[...] = cyr_ref[...]
        hi_ref[...] = cyi_ref[...]
    y = y_ref[...]
    z = jax.nn.gelu(y) * jax.nn.sigmoid(_dot(y.astype(BF16), wglu_ref[...]) + bglu_ref[...])
    zb = _dot(permt_ref[...], z.astype(BF16)).astype(BF16)
    m = _dot(zb, wout_ref[...])
    o_ref[...] = _layer_norm(alpha * x + m, g_ref[...], b_ref[...])


def s5_layer(x, h0r, h0i, w, g, b, alpha, *, nseg, lseg, cw, chained):
    m, d = x.shape
    tb = nseg * lseg
    n_state = h0r.shape[1]
    sw = S5_SLAB * STATE_N
    row = lambda i: (i, 0)
    hs = h0r.shape
    nat = jnp.arange(tb, dtype=I32)
    perm = ((nat[:, None] % nseg) * lseg + nat[:, None] // nseg == nat[None, :]).astype(BF16)
    scratch = [pltpu.VMEM((tb, d), F32), pltpu.VMEM((tb, d), F32), pltpu.VMEM((tb, sw), F32), pltpu.VMEM((tb, sw), F32)]
    if chained:
        scratch += [pltpu.VMEM((lseg, n_state), F32), pltpu.VMEM((lseg, n_state), F32),
                    pltpu.VMEM((1, n_state), F32), pltpu.VMEM((1, n_state), F32)]
    else:
        assert m == tb
    weights = [w["w_in"], w["bb_re"], w["bb_im"], w["c_re"], w["c_im"], w["a_re"], w["a_im"], w["d"],
               w["w_glu"], w["b_glu"], w["w_out"], g, b, h0r, h0i, perm, perm.T]
    return pl.pallas_call(
        functools.partial(_s5_body, alpha=alpha, nseg=nseg, lseg=lseg, cw=cw, chained=chained),
        grid=(m // tb,),
        in_specs=[pl.BlockSpec((tb, d), row)] + [_resident(a.shape) for a in weights],
        out_specs=[pl.BlockSpec((tb, d), row), pl.BlockSpec(hs, lambda i: (0, 0)), pl.BlockSpec(hs, lambda i: (0, 0))],
        out_shape=[jax.ShapeDtypeStruct((m, d), F32), jax.ShapeDtypeStruct(hs, F32), jax.ShapeDtypeStruct(hs, F32)],
        scratch_shapes=scratch, compiler_params=_params("arbitrary"), name="s5_layer")(x, *weights)


def s5_weights(w_in, lam_re, lam_im, log_dt, b_re, b_im, c_re, c_im, d, w_glu, b_glu, w_out):
    dt = jnp.exp(log_dt)[:, None]
    mag = jnp.exp(lam_re * dt)
    ang = lam_im * dt
    ab_re, ab_im = mag * jnp.cos(ang), mag * jnp.sin(ang)
    den = lam_re * lam_re + lam_im * lam_im
    f_re = ((ab_re - 1.0) * lam_re + ab_im * lam_im) / den
    f_im = (ab_im * lam_re - (ab_re - 1.0) * lam_im) / den
    bb_re = f_re[..., None] * b_re - f_im[..., None] * b_im
    bb_im = f_re[..., None] * b_im + f_im[..., None] * b_re
    n_g = lam_re.shape[0]
    eye = jnp.eye(S5_SLAB, dtype=F32)

    def in_slabs(bb):
        t = jnp.transpose(bb, (0, 2, 1)).reshape(n_g // S5_SLAB, S5_SLAB, GROUP_P, STATE_N)
        return jnp.einsum("sgpn,gh->sgphn", t, eye).reshape(n_g // S5_SLAB, S5_SLAB * GROUP_P, S5_SLAB * STATE_N).astype(BF16)

    def out_slabs(c):
        t = jnp.transpose(c, (0, 2, 1)).reshape(n_g // S5_SLAB, S5_SLAB, STATE_N, GROUP_P)
        return jnp.einsum("sgnp,gh->sgnhp", t, eye).reshape(n_g // S5_SLAB, S5_SLAB * STATE_N, S5_SLAB * GROUP_P).astype(BF16)

    return dict(w_in=w_in.astype(BF16), bb_re=in_slabs(bb_re), bb_im=in_slabs(bb_im), c_re=out_slabs(c_re),
                c_im=out_slabs(c_im), a_re=ab_re.reshape(1, -1), a_im=ab_im.reshape(1, -1), d=d.reshape(1, -1),
                w_glu=w_glu.astype(BF16), b_glu=b_glu.reshape(1, -1), w_out=w_out.astype(BF16))


def _sgate_body(pt_ref, q_ref, *rest, n_pg, nb):
    pages, idx_ref, km_ref = rest[:n_pg], rest[n_pg], rest[n_pg + 1]
    pc = pl.program_id(1)
    ppb = BLOCK // PAGE_SIZE
    lane = lax.broadcasted_iota(I32, km_ref.shape, 2)

    @pl.when(pc == 0)
    def _():
        km_ref[...] = jnp.zeros(km_ref.shape, F32)

    km = km_ref[...]
    for u in range(n_pg // ppb):
        tot = functools.reduce(jnp.add, [pages[u * ppb + t][0, 0] for t in range(ppb)])
        mean = jnp.sum(tot, axis=-1, keepdims=True) * (1.0 / BLOCK)
        km = jnp.where(lane == pc * (n_pg // ppb) + u, mean, km)
    km_ref[...] = km

    @pl.when(pc == pl.num_programs(1) - 1)
    def _():
        km = km_ref[...]
        bl = lax.broadcasted_iota(I32, (km.shape[0], km.shape[2]), 1)
        for qi in range(q_ref.shape[3]):
            gate = jnp.sum(km * q_ref[0, :, :, qi:qi + 1], axis=1)
            for t, am in enumerate(_top3(jnp.where(bl < nb, gate, -jnp.inf), bl, 1)):
                idx_ref[0, qi, :, t:t + 1] = am


def sample_gate(qt, cache_kt, page_table, li):
    db, nh, dh, nq = qt.shape
    n_pages = page_table.shape[1]
    nb = n_pages * PAGE_SIZE // BLOCK
    assert nb <= LANES
    n_pg = 8
    page = lambda t: pl.BlockSpec((1, 1, nh, dh, PAGE_SIZE),
                                  lambda b, pc, pt: (li, pt[b * n_pages + pc * n_pg + t], 0, 0, 0))
    return pl.pallas_call(
        functools.partial(_sgate_body, n_pg=n_pg, nb=nb),
        grid_spec=pltpu.PrefetchScalarGridSpec(
            num_scalar_prefetch=1, grid=(db, n_pages // n_pg),
            in_specs=[pl.BlockSpec((1, nh, dh, nq), lambda b, pc, pt: (b, 0, 0, 0))] + [page(t) for t in range(n_pg)],
            out_specs=pl.BlockSpec((1, nq, nh, TOPK), lambda b, pc, pt: (b, 0, 0, 0)),
            scratch_shapes=[pltpu.VMEM((nh, dh, LANES), F32)]),
        out_shape=jax.ShapeDtypeStruct((db, nq, nh, TOPK), I32),
        compiler_params=_params("arbitrary", "arbitrary"), name="sample_gate")(
            page_table.reshape(-1), qt, *([cache_kt] * n_pg))


def _sattn_body(pt_ref, idx_ref, thr_ref, tab_ref, q_ref, kn_ref, vn_ref, *rest, nq, past_len):
    n_pg = nq * TOPK * (BLOCK // PAGE_SIZE)
    kpg, vpg, o_ref = rest[:n_pg], rest[n_pg:2 * n_pg], rest[2 * n_pg]
    b = pl.program_id(0)
    h = pl.program_id(1)
    ppb = BLOCK // PAGE_SIZE
    far = thr_ref[N_BUCKETS - 1]
    last = tab_ref[(N_BUCKETS - 1) * N_HEADS + h]
    kn, vn = kn_ref[0, 0], vn_ref[0, 0]
    off = lax.broadcasted_iota(I32, (1, BLOCK), 1)
    d_new = lax.broadcasted_iota(I32, (nq, nq), 0) - lax.broadcasted_iota(I32, (nq, nq), 1)
    bias_new = _bias_of(d_new, h, thr_ref, tab_ref, unroll=True)
    for qi in range(nq):
        qcol = q_ref[0, 0, :, qi:qi + 1] * QK_SCALE
        scores = []
        for t in range(TOPK):
            blk = idx_ref[((b * nq + qi) * N_HEADS + h) * TOPK + t]
            dist = (past_len + qi) - (blk * BLOCK + off)
            bias = lax.cond((past_len + qi) - (blk * BLOCK + BLOCK - 1) >= far,
                            lambda: jnp.full((1, BLOCK), last, F32),
                            lambda dist=dist: _bias_of(dist, h, thr_ref, tab_ref, unroll=True))
            for pp in range(ppb):
                kt = kpg[(qi * TOPK + t) * ppb + pp][0, 0, 0]
                scores.append(jnp.sum(kt * qcol, axis=0, keepdims=True)
                              + bias[:, pp * PAGE_SIZE:(pp + 1) * PAGE_SIZE])
        s_own = jnp.sum(kn * qcol, axis=0, keepdims=True) + bias_new[qi:qi + 1, :]
        scores.append(jnp.where(d_new[qi:qi + 1, :] >= 0, s_own, NEG_INF))
        m = functools.reduce(jnp.maximum, [jnp.max(s, axis=1, keepdims=True) for s in scores])
        ps = [jnp.exp(s - m) for s in scores]
        l = functools.reduce(jnp.add, [jnp.sum(p, axis=1, keepdims=True) for p in ps])
        acc = functools.reduce(jnp.add, [vpg[qi * TOPK * ppb + j][0, 0, 0] * (ps[j] / l) for j in range(TOPK * ppb)])
        o = jnp.sum(acc, axis=1, keepdims=True) + jnp.sum(vn * (ps[-1] / l), axis=1, keepdims=True)
        o_ref[0, 0, :, qi:qi + 1] = o


def sample_attn(qt, knt, vnt, cache_kt, cache_vt, page_table, idx, thr, tab, li, past_len):
    db, nh, dh, nq = qt.shape
    n_pages = page_table.shape[1]
    ppb = BLOCK // PAGE_SIZE

    def page(qi, t, pp):
        def index(b, h, pt, ix, th):
            blk = ix[((b * nq + qi) * nh + h) * TOPK + t]
            return (li, pt[b * n_pages + blk * ppb + pp], h, 0, 0)
        return pl.BlockSpec((1, 1, 1, dh, PAGE_SIZE), index)

    pages = [page(qi, t, pp) for qi in range(nq) for t in range(TOPK) for pp in range(ppb)]
    tok = pl.BlockSpec((1, 1, dh, nq), lambda b, h, pt, ix, th: (b, h, 0, 0))
    return pl.pallas_call(
        functools.partial(_sattn_body, nq=nq, past_len=past_len),
        grid_spec=pltpu.PrefetchScalarGridSpec(
            num_scalar_prefetch=3, grid=(db, nh),
            in_specs=[pl.BlockSpec(memory_space=pltpu.SMEM), tok, tok, tok] + pages + pages,
            out_specs=tok),
        out_shape=jax.ShapeDtypeStruct((db, nh, dh, nq), F32),
        compiler_params=_params("arbitrary", "arbitrary"), name="sample_attn")(
            page_table.reshape(-1), idx.reshape(-1), thr, tab, qt, knt, vnt,
            *([cache_kt] * len(pages)), *([cache_vt] * len(pages)))


def _bucket_thresholds():
    n = jnp.arange(MAX_DISTANCE + 1, dtype=I32)
    nf = jnp.maximum(n, 1).astype(F32)
    log_b = MAX_EXACT + (jnp.log(nf / MAX_EXACT) / math.log(MAX_DISTANCE / MAX_EXACT)
                         * (N_BUCKETS - MAX_EXACT)).astype(I32)
    bucket = jnp.where(n < MAX_EXACT, n, jnp.minimum(log_b, N_BUCKETS - 1))
    return jnp.sum(bucket[None, :] < jnp.arange(N_BUCKETS, dtype=I32)[:, None], axis=1).astype(I32)


def _heads(t, b, s):
    return t.reshape(b, s, N_HEADS, HEAD_DIM).transpose(0, 2, 1, 3)


def kernel(x_prompt, x_sample, cache_k, cache_v, state_s5_re, state_s5_im, page_table, p_prompt, p_sample, ln_g, ln_b, ffn_w1, ffn_w3, ffn_w2, ple_w_proj, ple_w_gate, rel_bias_table, attn_w_qkv, attn_w_o, s5_w_in, s5_lambda_re, s5_lambda_im, s5_log_dt, s5_b_re, s5_b_im, s5_c_re, s5_c_im, s5_d, s5_w_glu, s5_b_glu, s5_w_out):
    depth = ln_g.shape[0]
    alpha = (2 * depth) ** 0.25
    bsz, seq, d = x_prompt.shape
    db, nq, _ = x_sample.shape
    assert bsz == 1 and seq % (2 * BLOCK) == 0 and d == N_HEADS * HEAD_DIM
    past_len = page_table.shape[1] * PAGE_SIZE
    assert past_len % BLOCK == 0
    xp = x_prompt.reshape(seq, d)
    xs = x_sample.reshape(db * nq, d)
    thr = _bucket_thresholds()
    tab = rel_bias_table.reshape(-1)
    vec = lambda a: a.reshape(1, -1)
    outs = {k: [] for k in ("kp", "vp", "ks", "vs", "spr", "spi", "ssr", "ssi")}
    for i in range(depth):
        li = i // 2
        w1, w3, w2 = ffn_w1[i].astype(BF16), ffn_w3[i].astype(BF16), ffn_w2[i].astype(BF16)
        g, b = ln_g[i], ln_b[i]
        xp = ffn_ln(xp, w1[0], w3[0], w2[0], vec(g[0]), vec(b[0]), alpha)
        xs = ffn_ln(xs, w1[0], w3[0], w2[0], vec(g[0]), vec(b[0]), alpha)
        if i % 2 == 0:
            wqkv = attn_w_qkv[li].astype(BF16)
            wq, wk, wv = wqkv[:, :d], wqkv[:, d:2 * d], wqkv[:, 2 * d:]
            wo = attn_w_o[li].astype(BF16)
            q, kto, vto, kta, vb, km = qkv_prompt(xp, wq, wv, wk.T, wv.T)
            kmt = jnp.transpose(km, (1, 0, 2)).reshape(d, seq // BLOCK)
            o = moba_prompt(q, kta, vb, kmt, thr, tab)
            xp = lin_ln(xp, o, wo, vec(g[1]), vec(b[1]), alpha)
            outs["kp"].append(jnp.swapaxes(kto.reshape(1, N_HEADS, HEAD_DIM, seq), 2, 3))
            outs["vp"].append(jnp.swapaxes(vto.reshape(1, N_HEADS, HEAD_DIM, seq), 2, 3))

            qkv_s = linear(xs, wqkv)
            qs, ks, vs = (_heads(t, db, nq) for t in jnp.split(qkv_s, 3, axis=-1))
            ckt, cvt = jnp.swapaxes(cache_k, 3, 4), jnp.swapaxes(cache_v, 3, 4)
            qt, knt, vnt = (jnp.swapaxes(t, 2, 3) for t in (qs, ks, vs))
            idx = sample_gate(qt, ckt, page_table, li)
            o_s = sample_attn(qt, knt, vnt, ckt, cvt, page_table, idx, thr, tab, li, past_len)
            xs = lin_ln(xs, o_s.transpose(0, 3, 1, 2).reshape(db * nq, d), wo, vec(g[1]), vec(b[1]), alpha)
            outs["ks"].append(ks)
            outs["vs"].append(vs)
        else:
            w = s5_weights(s5_w_in[li], s5_lambda_re[li], s5_lambda_im[li], s5_log_dt[li], s5_b_re[li], s5_b_im[li],
                           s5_c_re[li], s5_c_im[li], s5_d[li], s5_w_glu[li], s5_b_glu[li], s5_w_out[li])
            n_state = w["a_re"].shape[1]
            zero = jnp.zeros((1, n_state), F32)
            xp, hr, hi = s5_layer(xp, zero, zero, w, vec(g[1]), vec(b[1]), alpha, nseg=8, lseg=64, cw=1024, chained=True)
            outs["spr"].append(hr.reshape(1, -1, STATE_N))
            outs["spi"].append(hi.reshape(1, -1, STATE_N))
            xs, hr, hi = s5_layer(xs, state_s5_re[li].reshape(db, n_state), state_s5_im[li].reshape(db, n_state), w,
                                  vec(g[1]), vec(b[1]), alpha, nseg=db, lseg=nq, cw=256, chained=False)
            outs["ssr"].append(hr.reshape(db, -1, STATE_N))
            outs["ssi"].append(hi.reshape(db, -1, STATE_N))
        ple_w = (ple_w_gate[i].astype(BF16), ple_w_proj[i].astype(BF16))
        xp = ffn_ln(xp, w1[1], w3[1], w2[1], vec(g[2]), vec(b[2]), alpha, ple=(p_prompt[i].reshape(seq, -1),) + ple_w)
        xs = ffn_ln(xs, w1[1], w3[1], w2[1], vec(g[2]), vec(b[2]), alpha, ple=(p_sample[i].reshape(db * nq, -1),) + ple_w)
    st = lambda k: jnp.stack(outs[k])
    return (xp.reshape(bsz, seq, d), xs.reshape(db, nq, d), st("kp"), st("vp"), st("ks"), st("vs"),
            st("spr"), st("spi"), st("ssr"), st("ssi"))
```

```python
import functools
import math

import jax
import jax.numpy as jnp
from jax import lax
from jax.experimental import pallas as pl
from jax.experimental.pallas import tpu as pltpu

F32 = jnp.float32
BF16 = jnp.bfloat16
I32 = jnp.int32

N_HEADS = 16
HEAD_DIM = 64
BLOCK = 256
TOPK = 3
PAGE_SIZE = 128
N_BUCKETS = 32
MAX_EXACT = N_BUCKETS // 2
MAX_DISTANCE = 4096
GROUP_P = 16
STATE_N = 64
LN_EPS = 1e-5
NEG_INF = -1e30
MASK_BIG = 1e30
QK_SCALE = HEAD_DIM ** -0.5

LANES = 128
VMEM_LIMIT = 56 * 1024 * 1024

N_NEAR = MAX_DISTANCE // BLOCK + 1
Q_TILE = 2 * BLOCK
N_FAR_PAIR = (N_NEAR + 2) // 2
S5_SLAB = 16


def _params(*sem, flags=None):
    return pltpu.CompilerParams(dimension_semantics=sem, vmem_limit_bytes=VMEM_LIMIT, flags=flags)


def _resident(shape):
    nd = len(shape)
    return pl.BlockSpec(shape, lambda *_: (0,) * nd, pipeline_mode=pl.Buffered(1))


def _dot(a, b):
    return jnp.dot(a, b, preferred_element_type=F32)


def _layer_norm(y, g, b):
    mu = jnp.mean(y, axis=-1, keepdims=True)
    yc = y - mu
    var = jnp.mean(yc * yc, axis=-1, keepdims=True)
    return yc * lax.rsqrt(var + LN_EPS) * g + b


def _top3(work, cols, axis):
    out = []
    big = jnp.int32(1 << 30)
    for _ in range(TOPK):
        mx = jnp.max(work, axis=axis, keepdims=True)
        am = jnp.min(jnp.where(work == mx, cols, big), axis=axis, keepdims=True)
        out.append(jnp.where(mx > -jnp.inf, am, -1))
        work = jnp.where(cols == am, -jnp.inf, work)
    return out


def _bias_of(dist, head, thr_ref, tab_ref, unroll=False):
    def step(b, acc):
        return jnp.where(dist >= thr_ref[b], tab_ref[b * N_HEADS + head], acc)
    acc = jnp.full(dist.shape, tab_ref[head], F32)
    if unroll:
        for b in range(1, N_BUCKETS):
            acc = step(b, acc)
        return acc
    return lax.fori_loop(1, N_BUCKETS, step, acc)


def _ffn_body(x_ref, w1_ref, w3_ref, w2_ref, g_ref, b_ref, *rest, alpha, tf, ple):
    if ple:
        p_ref, wg_ref, wp_ref, o_ref = rest
    else:
        (o_ref,) = rest
    x = x_ref[...]
    xb = x.astype(BF16)
    acc = jnp.zeros(x.shape, F32)
    for j in range(w1_ref.shape[1] // tf):
        sl = slice(j * tf, (j + 1) * tf)
        h = jax.nn.silu(_dot(xb, w1_ref[:, sl])) * _dot(xb, w3_ref[:, sl])
        acc = acc + _dot(h.astype(BF16), w2_ref[sl, :])
    y = _layer_norm(alpha * x + 0.5 * acc, g_ref[...], b_ref[...])
    if ple:
        gate = jax.nn.sigmoid(_dot(y.astype(BF16), wg_ref[...]))
        y = y + gate * _dot(p_ref[...].astype(BF16), wp_ref[...])
    o_ref[...] = y


def ffn_ln(x, w1, w3, w2, g, b, alpha, ple=None):
    m, d = x.shape
    tm = min(512, m)
    row = lambda i: (i, 0)
    in_specs = [pl.BlockSpec((tm, d), row), _resident(w1.shape), _resident(w3.shape), _resident(w2.shape),
                _resident(g.shape), _resident(b.shape)]
    args = [x, w1, w3, w2, g, b]
    if ple is not None:
        p, wg, wp = ple
        in_specs += [pl.BlockSpec((tm, p.shape[1]), row), _resident(wg.shape), _resident(wp.shape)]
        args += [p, wg, wp]
    return pl.pallas_call(
        functools.partial(_ffn_body, alpha=alpha, tf=2 * LANES, ple=ple is not None),
        grid=(m // tm,), in_specs=in_specs, out_specs=pl.BlockSpec((tm, d), row),
        out_shape=jax.ShapeDtypeStruct((m, d), F32), compiler_params=_params("parallel"),
        name="ffn_ln")(*args)


def _lin_ln_body(x_ref, a_ref, w_ref, g_ref, b_ref, o_ref, *, alpha):
    m = _dot(a_ref[...].astype(BF16), w_ref[...])
    o_ref[...] = _layer_norm(alpha * x_ref[...] + m, g_ref[...], b_ref[...])


def lin_ln(x, a, w, g, b, alpha):
    m, d = x.shape
    tm = min(512, m)
    row = lambda i: (i, 0)
    return pl.pallas_call(
        functools.partial(_lin_ln_body, alpha=alpha),
        grid=(m // tm,),
        in_specs=[pl.BlockSpec((tm, d), row), pl.BlockSpec((tm, a.shape[1]), row), _resident(w.shape),
                  _resident(g.shape), _resident(b.shape)],
        out_specs=pl.BlockSpec((tm, d), row), out_shape=jax.ShapeDtypeStruct((m, d), F32),
        compiler_params=_params("parallel"), name="lin_ln")(x, a, w, g, b)


def _linear_body(x_ref, w_ref, o_ref):
    o_ref[...] = _dot(x_ref[...].astype(BF16), w_ref[...])


def linear(x, w):
    m = x.shape[0]
    return pl.pallas_call(
        _linear_body, grid=(1,), in_specs=[_resident(x.shape), _resident(w.shape)],
        out_specs=pl.BlockSpec((m, w.shape[1]), lambda i: (0, 0)),
        out_shape=jax.ShapeDtypeStruct((m, w.shape[1]), F32), compiler_params=_params("arbitrary"),
        name="linear")(x, w)


def _qkv_body(x_ref, wq_ref, wv_ref, wkt_ref, wvt_ref, q_ref, kto_ref, vto_ref, kta_ref, vb_ref, km_ref):
    xb = x_ref[...].astype(BF16)
    tm, d = xb.shape
    nblk = tm // BLOCK
    q_ref[...] = _dot(xb, wq_ref[...])
    vb_ref[...] = _dot(xb, wv_ref[...]).astype(BF16)
    nt = lambda w_ref: lax.dot_general(w_ref[...], xb, (((1,), (1,)), ((), ())), preferred_element_type=F32)
    kt = nt(wkt_ref)
    kto_ref[...] = kt
    vto_ref[...] = nt(wvt_ref)
    ktb = kt.astype(BF16)
    row = lax.broadcasted_iota(I32, (LANES, BLOCK), 0)
    for bk in range(nblk):
        n = pl.program_id(0) * nblk + bk
        cols = slice(bk * BLOCK, (bk + 1) * BLOCK)
        km_ref[0, :, bk:bk + 1] = jnp.sum(kt[:, cols], axis=1, keepdims=True) * (1.0 / BLOCK)
        mask_rows = jnp.where(row % HEAD_DIM == n, MASK_BIG, 0.0).astype(BF16)
        for p in range(d // LANES):
            pair = ktb[p * LANES:(p + 1) * LANES, cols]
            kta_ref[2 * p, bk] = jnp.where(row < HEAD_DIM, pair, mask_rows)
            kta_ref[2 * p + 1, bk] = jnp.where(row >= HEAD_DIM, pair, mask_rows)


def qkv_prompt(x, wq, wv, wkt, wvt):
    s, d = x.shape
    tm = 2 * BLOCK
    nb = s // BLOCK
    assert nb <= HEAD_DIM
    row = lambda i: (i, 0)
    col = lambda i: (0, i)
    return pl.pallas_call(
        _qkv_body, grid=(s // tm,),
        in_specs=[pl.BlockSpec((tm, d), row)] + [_resident(w.shape) for w in (wq, wv, wkt, wvt)],
        out_specs=[pl.BlockSpec((tm, d), row),
                   pl.BlockSpec((d, tm), col),
                   pl.BlockSpec((d, tm), col),
                   pl.BlockSpec((N_HEADS, tm // BLOCK, LANES, BLOCK), lambda i: (0, i, 0, 0)),
                   pl.BlockSpec((tm, d), row),
                   pl.BlockSpec((1, d, tm // BLOCK), lambda i: (i, 0, 0))],
        out_shape=[jax.ShapeDtypeStruct((s, d), F32),
                   jax.ShapeDtypeStruct((d, s), F32),
                   jax.ShapeDtypeStruct((d, s), F32),
                   jax.ShapeDtypeStruct((N_HEADS, nb, LANES, BLOCK), BF16),
                   jax.ShapeDtypeStruct((s, d), BF16),
                   jax.ShapeDtypeStruct((s // tm, d, tm // BLOCK), F32)],
        compiler_params=_params("parallel"), name="qkv_prompt")(x, wq, wv, wkt, wvt)


def _split_bf16(x):
    hi = x.astype(BF16)
    return hi, (x - hi.astype(F32)).astype(BF16)


def _moba_body(thr_ref, tab_ref, q_ref, kt_ref, v_ref, km_ref, o_ref, bias_ref, s_ref, p_ref):
    hp = pl.program_id(0)
    it = pl.program_id(1)
    nb = km_ref.shape[1]
    rr = lax.broadcasted_iota(I32, (BLOCK, BLOCK), 0)
    cc = lax.broadcasted_iota(I32, (BLOCK, BLOCK), 1)

    @pl.when(it == 0)
    def _():
        for hh in range(2):
            head = hp * 2 + hh

            def fill(dl, carry):
                bias_ref[hh, dl] = _bias_of(rr - cc + dl * BLOCK, head, thr_ref, tab_ref)
                return carry
            lax.fori_loop(0, N_NEAR, fill, 0)
            bias_ref[hh, N_NEAR] = jnp.full((BLOCK, BLOCK), tab_ref[(N_BUCKETS - 1) * N_HEADS + head], F32)

    lane = lax.broadcasted_iota(I32, (Q_TILE, LANES), 1)
    own = 2 * it + lax.broadcasted_iota(I32, (Q_TILE, 1), 0) // BLOCK
    bcol = lax.broadcasted_iota(I32, (Q_TILE, nb), 1)
    causal = rr >= cc
    q = q_ref[...]
    km_hi, km_lo = _split_bf16(km_ref[...])

    def keys(hh, blk):
        return kt_ref[hh, blk]

    vlane = lax.broadcasted_iota(I32, (Q_TILE, LANES), 1)

    def attend(hh, p, pair):
        v = v_ref[pl.ds(pl.multiple_of(pair * Q_TILE, Q_TILE), Q_TILE), :]
        mine = (vlane >= hh * HEAD_DIM) & (vlane < (hh + 1) * HEAD_DIM)
        return _dot(p, jnp.where(mine, v, jnp.ones_like(v)))

    lhs, state, c_far = [], [], []
    for hh in range(2):
        in_q = (lane >= hh * HEAD_DIM) & (lane < (hh + 1) * HEAD_DIM)
        qh = jnp.where(in_q, q, 0.0)
        q_hi, q_lo = _split_bf16(qh)
        gate = _dot(q_hi, km_hi) + _dot(q_hi, km_lo) + _dot(q_lo, km_hi)
        i1, i2, i3 = _top3(jnp.where(bcol < own, gate, -jnp.inf), bcol, 1)
        e = lane - (1 - hh) * HEAD_DIM
        unsel = jnp.where((e == i1) | (e == i2) | (e == i3), 0.0, -1.0)
        qs = qh * QK_SCALE
        lhs.append(jnp.where(in_q, qs, unsel).astype(BF16))
        c_far.append(tab_ref[(N_BUCKETS - 1) * N_HEADS + hp * 2 + hh])

        lhs_own = jnp.where(in_q, qs, jnp.where(e == own, 0.0, unsel)).astype(BF16)
        ss = []
        for b in range(2):
            s = _dot(lhs_own, keys(hh, 2 * it + b))
            top = s[:BLOCK] + bias_ref[hh, 0]
            bot = s[BLOCK:] + bias_ref[hh, 1 - b]
            if b == 0:
                top = jnp.where(causal, top, NEG_INF)
            else:
                bot = jnp.where(causal, bot, NEG_INF)
            ss.append(jnp.concatenate([top, bot], axis=0))
        m = jnp.max(jnp.maximum(ss[0], ss[1]), axis=-1, keepdims=True)
        p = jnp.concatenate([jnp.exp(s - m).astype(BF16) for s in ss], axis=1)
        state += [m, attend(hh, p, it)]

    def step(jp, carry, near, slot=0):
        for hh in range(2):
            for b in range(2):
                s_ref[slot, hh, :, b * BLOCK:(b + 1) * BLOCK] = _dot(lhs[hh], keys(hh, 2 * jp + b))
        out = []
        for hh in range(2):
            m, acc = carry[2 * hh:2 * hh + 2]

            def scores(halves):
                s = (jnp.concatenate([s_ref[slot, hh, :, :BLOCK], s_ref[slot, hh, :, BLOCK:]], axis=1) if halves
                     else s_ref[slot, hh])
                if near:
                    dl = 2 * (it - jp)
                    tile = lambda a, b: bias_ref[hh, jnp.minimum(dl + a - b, N_NEAR)]
                    s = s + jnp.concatenate([jnp.concatenate([tile(0, 0), tile(0, 1)], axis=1),
                                             jnp.concatenate([tile(1, 0), tile(1, 1)], axis=1)], axis=0)
                return s

            rmax = jnp.max(scores(False), axis=-1, keepdims=True)
            if near:
                m_new = jnp.maximum(m, rmax)
                shift = m_new
            else:
                m_new = jnp.maximum(m, rmax + c_far[hh])
                shift = m_new - c_far[hh]
            p_ref[slot, hh] = jnp.exp(scores(True) - shift).astype(BF16)
            out += [m_new, jnp.exp(m - m_new) * acc + attend(hh, p_ref[slot, hh], jp)]
        return tuple(out)

    def run(lo, hi, carry, near):
        def two(t, c):
            return step(lo + 2 * t + 1, step(lo + 2 * t, c, near, 0), near, 1)
        n2 = (hi - lo) // 2
        carry = lax.fori_loop(0, n2, two, carry)
        return lax.fori_loop(lo + 2 * n2, hi, functools.partial(step, near=near), carry)

    n_far = jnp.maximum(it - (N_FAR_PAIR - 1), 0)
    state = run(0, n_far, tuple(state), False)
    state = run(n_far, it, state, True)
    l0 = state[1][:, HEAD_DIM:HEAD_DIM + 1]
    l1 = state[3][:, 0:1]
    o_ref[...] = jnp.where(lane < HEAD_DIM, state[1] / l0, state[3] / l1).astype(BF16)


def moba_prompt(q, kta, vb, kmt, thr, tab):
    s, d = q.shape
    nb = s // BLOCK
    smem = pl.BlockSpec(memory_space=pltpu.SMEM)
    return pl.pallas_call(
        _moba_body, grid=(d // LANES, s // Q_TILE),
        in_specs=[smem, smem,
                  pl.BlockSpec((Q_TILE, LANES), lambda hp, i: (i, hp)),
                  pl.BlockSpec((2, nb, LANES, BLOCK), lambda hp, i: (hp, 0, 0, 0)),
                  pl.BlockSpec((s, LANES), lambda hp, i: (0, hp)),
                  pl.BlockSpec((LANES, nb), lambda hp, i: (hp, 0))],
        out_specs=pl.BlockSpec((Q_TILE, LANES), lambda hp, i: (i, hp)),
        out_shape=jax.ShapeDtypeStruct((s, d), BF16),
        scratch_shapes=[pltpu.VMEM((2, N_NEAR + 1, BLOCK, BLOCK), F32), pltpu.VMEM((2, 2, Q_TILE, Q_TILE), F32),
                        pltpu.VMEM((2, 2, Q_TILE, Q_TILE), BF16)],
        compiler_params=_params("arbitrary", "arbitrary"),
        name="moba_prompt")(thr, tab, q, kta, vb, kmt)


def _s5_body(x_ref, win_ref, bbr_ref, bbi_ref, cr_ref, ci_ref, ar_ref, ai_ref, d_ref, wglu_ref, bglu_ref,
             wout_ref, g_ref, b_ref, h0r_ref, h0i_ref, perm_ref, permt_ref, o_ref, hr_ref, hi_ref,
             u_ref, y_ref, sr_ref, si_ref, *chain_scratch, alpha, nseg, lseg, cw, chained):
    step = pl.program_id(0)
    slab = S5_SLAB * GROUP_P
    sw = S5_SLAB * STATE_N
    n_slab = win_ref.shape[1] // slab
    rows = lambda j: pl.ds(pl.multiple_of(j * nseg, 8), nseg)

    if chained:
        pwr_ref, pwi_ref, cyr_ref, cyi_ref = chain_scratch

        @pl.when(step == 0)
        def _():
            cyr_ref[...] = h0r_ref[...]
            cyi_ref[...] = h0i_ref[...]
            for c in range(ar_ref.shape[1] // sw):
                cs = slice(c * sw, (c + 1) * sw)
                ar, ai = ar_ref[:, cs], ai_ref[:, cs]

                def power(j, carry):
                    pr, pi = carry
                    pwr_ref[pl.ds(j, 1), cs] = pr
                    pwi_ref[pl.ds(j, 1), cs] = pi
                    return pr * ar - pi * ai, pr * ai + pi * ar
                lax.fori_loop(0, lseg, power, (ar, ai))

    x = x_ref[...]
    xb = _dot(perm_ref[...], x.astype(BF16)).astype(BF16)
    u_ref[...] = _dot(xb, win_ref[...])
    for s in range(n_slab):
        ub = u_ref[:, s * slab:(s + 1) * slab].astype(BF16)
        sr_ref[...] = _dot(ub, bbr_ref[s])
        si_ref[...] = _dot(ub, bbi_ref[s])
        for c in range(sw // cw):
            cs = slice(c * cw, (c + 1) * cw)
            gs = slice(s * sw + c * cw, s * sw + (c + 1) * cw)
            ar = jnp.broadcast_to(ar_ref[:, gs], (nseg, cw))
            ai = jnp.broadcast_to(ai_ref[:, gs], (nseg, cw))

            def scan(j, carry):
                hr, hi = carry
                nr = ar * hr - ai * hi + sr_ref[rows(j), cs]
                ni = ar * hi + ai * hr + si_ref[rows(j), cs]
                sr_ref[rows(j), cs] = nr
                si_ref[rows(j), cs] = ni
                return nr, ni

            if not chained:
                er, ei = lax.fori_loop(0, lseg, scan, (h0r_ref[:, gs], h0i_ref[:, gs]))
                hr_ref[:, gs] = er
                hi_ref[:, gs] = ei
            else:
                zero = jnp.zeros((nseg, cw), F32)
                er, ei = lax.fori_loop(0, lseg, scan, (zero, zero))
                plr, pli = pwr_ref[lseg - 1:lseg, gs], pwi_ref[lseg - 1:lseg, gs]
                cr, ci = cyr_ref[:, gs], cyi_ref[:, gs]
                hsr, hsi = [], []
                for r in range(nseg):
                    hsr.append(cr)
                    hsi.append(ci)
                    cr, ci = (er[r:r + 1] + plr * cr - pli * ci, ei[r:r + 1] + plr * ci + pli * cr)
                cyr_ref[:, gs] = cr
                cyi_ref[:, gs] = ci
                hsr = jnp.concatenate(hsr, axis=0)
                hsi = jnp.concatenate(hsi, axis=0)

                def fix(j, carry):
                    pr = jnp.broadcast_to(pwr_ref[pl.ds(j, 1), gs], (nseg, cw))
                    pi = jnp.broadcast_to(pwi_ref[pl.ds(j, 1), gs], (nseg, cw))
                    sr_ref[rows(j), cs] = sr_ref[rows(j), cs] + (pr * hsr - pi * hsi)
                    si_ref[rows(j), cs] = si_ref[rows(j), cs] + (pr * hsi + pi * hsr)
                    return carry
                lax.fori_loop(0, lseg, fix, 0)
        ys = _dot(sr_ref[...].astype(BF16), cr_ref[s]) - _dot(si_ref[...].astype(BF16), ci_ref[s])
        us = slice(s * slab, (s + 1) * slab)
        y_ref[:, us] = ys + d_ref[:, us] * u_ref[:, us]
    if chained:
        hr_ref[...] = cyr_ref[...]
        hi_ref[...] = cyi_ref[...]
    y = y_ref[...]
    z = jax.nn.gelu(y) * jax.nn.sigmoid(_dot(y.astype(BF16), wglu_ref[...]) + bglu_ref[...])
    zb = _dot(permt_ref[...], z.astype(BF16)).astype(BF16)
    m = _dot(zb, wout_ref[...])
    o_ref[...] = _layer_norm(alpha * x + m, g_ref[...], b_ref[...])


def s5_layer(x, h0r, h0i, w, g, b, alpha, *, nseg, lseg, cw, chained):
    m, d = x.shape
    tb = nseg * lseg
    n_state = h0r.shape[1]
    sw = S5_SLAB * STATE_N
    row = lambda i: (i, 0)
    hs = h0r.shape
    nat = jnp.arange(tb, dtype=I32)
    perm = ((nat[:, None] % nseg) * lseg + nat[:, None] // nseg == nat[None, :]).astype(BF16)
    scratch = [pltpu.VMEM((tb, d), F32), pltpu.VMEM((tb, d), F32), pltpu.VMEM((tb, sw), F32), pltpu.VMEM((tb, sw), F32)]
    if chained:
        scratch += [pltpu.VMEM((lseg, n_state), F32), pltpu.VMEM((lseg, n_state), F32),
                    pltpu.VMEM((1, n_state), F32), pltpu.VMEM((1, n_state), F32)]
    else:
        assert m == tb
    weights = [w["w_in"], w["bb_re"], w["bb_im"], w["c_re"], w["c_im"], w["a_re"], w["a_im"], w["d"],
               w["w_glu"], w["b_glu"], w["w_out"], g, b, h0r, h0i, perm, perm.T]
    return pl.pallas_call(
        functools.partial(_s5_body, alpha=alpha, nseg=nseg, lseg=lseg, cw=cw, chained=chained),
        grid=(m // tb,),
        in_specs=[pl.BlockSpec((tb, d), row)] + [_resident(a.shape) for a in weights],
        out_specs=[pl.BlockSpec((tb, d), row), pl.BlockSpec(hs, lambda i: (0, 0)), pl.BlockSpec(hs, lambda i: (0, 0))],
        out_shape=[jax.ShapeDtypeStruct((m, d), F32), jax.ShapeDtypeStruct(hs, F32), jax.ShapeDtypeStruct(hs, F32)],
        scratch_shapes=scratch, compiler_params=_params("arbitrary"), name="s5_layer")(x, *weights)


def s5_weights(w_in, lam_re, lam_im, log_dt, b_re, b_im, c_re, c_im, d, w_glu, b_glu, w_out):
    dt = jnp.exp(log_dt)[:, None]
    mag = jnp.exp(lam_re * dt)
    ang = lam_im * dt
    ab_re, ab_im = mag * jnp.cos(ang), mag * jnp.sin(ang)
    den = lam_re * lam_re + lam_im * lam_im
    f_re = ((ab_re - 1.0) * lam_re + ab_im * lam_im) / den
    f_im = (ab_im * lam_re - (ab_re - 1.0) * lam_im) / den
    bb_re = f_re[..., None] * b_re - f_im[..., None] * b_im
    bb_im = f_re[..., None] * b_im + f_im[..., None] * b_re
    n_g = lam_re.shape[0]
    eye = jnp.eye(S5_SLAB, dtype=F32)

    def in_slabs(bb):
        t = jnp.transpose(bb, (0, 2, 1)).reshape(n_g // S5_SLAB, S5_SLAB, GROUP_P, STATE_N)
        return jnp.einsum("sgpn,gh->sgphn", t, eye).reshape(n_g // S5_SLAB, S5_SLAB * GROUP_P, S5_SLAB * STATE_N).astype(BF16)

    def out_slabs(c):
        t = jnp.transpose(c, (0, 2, 1)).reshape(n_g // S5_SLAB, S5_SLAB, STATE_N, GROUP_P)
        return jnp.einsum("sgnp,gh->sgnhp", t, eye).reshape(n_g // S5_SLAB, S5_SLAB * STATE_N, S5_SLAB * GROUP_P).astype(BF16)

    return dict(w_in=w_in.astype(BF16), bb_re=in_slabs(bb_re), bb_im=in_slabs(bb_im), c_re=out_slabs(c_re),
                c_im=out_slabs(c_im), a_re=ab_re.reshape(1, -1), a_im=ab_im.reshape(1, -1), d=d.reshape(1, -1),
                w_glu=w_glu.astype(BF16), b_glu=b_glu.reshape(1, -1), w_out=w_out.astype(BF16))


def _sgate_body(pt_ref, q_ref, *rest, n_pg, nb):
    pages, idx_ref, km_ref = rest[:n_pg], rest[n_pg], rest[n_pg + 1]
    pc = pl.program_id(1)
    ppb = BLOCK // PAGE_SIZE
    lane = lax.broadcasted_iota(I32, km_ref.shape, 2)

    @pl.when(pc == 0)
    def _():
        km_ref[...] = jnp.zeros(km_ref.shape, F32)

    km = km_ref[...]
    for u in range(n_pg // ppb):
        tot = functools.reduce(jnp.add, [pages[u * ppb + t][0, 0] for t in range(ppb)])
        mean = jnp.sum(tot, axis=-1, keepdims=True) * (1.0 / BLOCK)
        km = jnp.where(lane == pc * (n_pg // ppb) + u, mean, km)
    km_ref[...] = km

    @pl.when(pc == pl.num_programs(1) - 1)
    def _():
        km = km_ref[...]
        bl = lax.broadcasted_iota(I32, (km.shape[0], km.shape[2]), 1)
        for qi in range(q_ref.shape[3]):
            gate = jnp.sum(km * q_ref[0, :, :, qi:qi + 1], axis=1)
            for t, am in enumerate(_top3(jnp.where(bl < nb, gate, -jnp.inf), bl, 1)):
                idx_ref[0, qi, :, t:t + 1] = am


def sample_gate(qt, cache_kt, page_table, li):
    db, nh, dh, nq = qt.shape
    n_pages = page_table.shape[1]
    nb = n_pages * PAGE_SIZE // BLOCK
    assert nb <= LANES
    n_pg = 8
    page = lambda t: pl.BlockSpec((1, 1, nh, dh, PAGE_SIZE),
                                  lambda b, pc, pt: (li, pt[b * n_pages + pc * n_pg + t], 0, 0, 0))
    return pl.pallas_call(
        functools.partial(_sgate_body, n_pg=n_pg, nb=nb),
        grid_spec=pltpu.PrefetchScalarGridSpec(
            num_scalar_prefetch=1, grid=(db, n_pages // n_pg),
            in_specs=[pl.BlockSpec((1, nh, dh, nq), lambda b, pc, pt: (b, 0, 0, 0))] + [page(t) for t in range(n_pg)],
            out_specs=pl.BlockSpec((1, nq, nh, TOPK), lambda b, pc, pt: (b, 0, 0, 0)),
            scratch_shapes=[pltpu.VMEM((nh, dh, LANES), F32)]),
        out_shape=jax.ShapeDtypeStruct((db, nq, nh, TOPK), I32),
        compiler_params=_params("arbitrary", "arbitrary"), name="sample_gate")(
            page_table.reshape(-1), qt, *([cache_kt] * n_pg))


def _sattn_body(pt_ref, idx_ref, thr_ref, tab_ref, q_ref, kn_ref, vn_ref, ck_hbm, cv_hbm, o_ref,
                kbuf, vbuf, sem, *, nq, nh, n_pages, li, past_len):
    ppb = BLOCK // PAGE_SIZE
    b = pl.program_id(0)
    far = thr_ref[N_BUCKETS - 1]
    off = lax.broadcasted_iota(I32, (1, BLOCK), 1)
    d_new = lax.broadcasted_iota(I32, (nq, nq), 0) - lax.broadcasted_iota(I32, (nq, nq), 1)

    def page_copies(bb, h, slot, fetch):
        out = []
        for qi in range(nq):
            for t in range(TOPK):
                blk = idx_ref[((bb * nq + qi) * nh + h) * TOPK + t] if fetch else 0
                for pp in range(ppb):
                    page = pt_ref[bb * n_pages + blk * ppb + pp] if fetch else 0
                    j = (qi * TOPK + t) * ppb + pp
                    out.append(pltpu.make_async_copy(ck_hbm.at[li, page, h], kbuf.at[slot, j], sem.at[slot]))
                    out.append(pltpu.make_async_copy(cv_hbm.at[li, page, h], vbuf.at[slot, j], sem.at[slot]))
        return out

    @pl.when(b == 0)
    def _():
        for c in page_copies(0, 0, 0, True):
            c.start()

    def head(h, carry):
        slot = h % 2
        last_head = h == nh - 1
        nb_, nh_ = jnp.where(last_head, b + 1, b), jnp.where(last_head, 0, h + 1)

        @pl.when(nb_ < pl.num_programs(0))
        def _():
            for c in page_copies(nb_, nh_, 1 - slot, True):
                c.start()

        for c in page_copies(0, 0, slot, False):
            c.wait()

        last = tab_ref[(N_BUCKETS - 1) * N_HEADS + h]
        kn, vn = kn_ref[0, h], vn_ref[0, h]
        bias_new = _bias_of(d_new, h, thr_ref, tab_ref, unroll=True)
        for qi in range(nq):
            qcol = q_ref[0, h, :, qi:qi + 1] * QK_SCALE
            scores = []
            for t in range(TOPK):
                blk = idx_ref[((b * nq + qi) * nh + h) * TOPK + t]
                dist = (past_len + qi) - (blk * BLOCK + off)
                bias = lax.cond((past_len + qi) - (blk * BLOCK + BLOCK - 1) >= far,
                                lambda: jnp.full((1, BLOCK), last, F32),
                                lambda dist=dist: _bias_of(dist, h, thr_ref, tab_ref, unroll=True))
                for pp in range(ppb):
                    kt = kbuf[slot, (qi * TOPK + t) * ppb + pp]
                    scores.append(jnp.sum(kt * qcol, axis=0, keepdims=True)
                                  + bias[:, pp * PAGE_SIZE:(pp + 1) * PAGE_SIZE])
            s_own = jnp.sum(kn * qcol, axis=0, keepdims=True) + bias_new[qi:qi + 1, :]
            scores.append(jnp.where(d_new[qi:qi + 1, :] >= 0, s_own, NEG_INF))
            m = functools.reduce(jnp.maximum, [jnp.max(s, axis=1, keepdims=True) for s in scores])
            ps = [jnp.exp(s - m) for s in scores]
            l = functools.reduce(jnp.add, [jnp.sum(p, axis=1, keepdims=True) for p in ps])
            acc = functools.reduce(jnp.add, [vbuf[slot, qi * TOPK * ppb + j] * (ps[j] / l) for j in range(TOPK * ppb)])
            o = jnp.sum(acc, axis=1, keepdims=True) + jnp.sum(vn * (ps[-1] / l), axis=1, keepdims=True)
            o_ref[0, h, :, qi:qi + 1] = o
        return carry

    lax.fori_loop(0, nh, head, 0)


def sample_attn(qt, knt, vnt, cache_kt, cache_vt, page_table, idx, thr, tab, li, past_len):
    db, nh, dh, nq = qt.shape
    assert nh % 2 == 0
    n_pages = page_table.shape[1]
    n_pg = nq * TOPK * (BLOCK // PAGE_SIZE)
    tok = pl.BlockSpec((1, nh, dh, nq), lambda b, pt, ix, th: (b, 0, 0, 0))
    hbm = pl.BlockSpec(memory_space=pl.ANY)
    return pl.pallas_call(
        functools.partial(_sattn_body, nq=nq, nh=nh, n_pages=n_pages, li=li, past_len=past_len),
        grid_spec=pltpu.PrefetchScalarGridSpec(
            num_scalar_prefetch=3, grid=(db,),
            in_specs=[pl.BlockSpec(memory_space=pltpu.SMEM), tok, tok, tok, hbm, hbm],
            out_specs=tok,
            scratch_shapes=[pltpu.VMEM((2, n_pg, dh, PAGE_SIZE), F32), pltpu.VMEM((2, n_pg, dh, PAGE_SIZE), F32),
                            pltpu.SemaphoreType.DMA((2,))]),
        out_shape=jax.ShapeDtypeStruct((db, nh, dh, nq), F32),
        compiler_params=_params("arbitrary"), name="sample_attn")(
            page_table.reshape(-1), idx.reshape(-1), thr, tab, qt, knt, vnt, cache_kt, cache_vt)


def _bucket_thresholds():
    n = jnp.arange(MAX_DISTANCE + 1, dtype=I32)
    nf = jnp.maximum(n, 1).astype(F32)
    log_b = MAX_EXACT + (jnp.log(nf / MAX_EXACT) / math.log(MAX_DISTANCE / MAX_EXACT)
                         * (N_BUCKETS - MAX_EXACT)).astype(I32)
    bucket = jnp.where(n < MAX_EXACT, n, jnp.minimum(log_b, N_BUCKETS - 1))
    return jnp.sum(bucket[None, :] < jnp.arange(N_BUCKETS, dtype=I32)[:, None], axis=1).astype(I32)


def _heads(t, b, s):
    return t.reshape(b, s, N_HEADS, HEAD_DIM).transpose(0, 2, 1, 3)


def kernel(x_prompt, x_sample, cache_k, cache_v, state_s5_re, state_s5_im, page_table, p_prompt, p_sample, ln_g, ln_b, ffn_w1, ffn_w3, ffn_w2, ple_w_proj, ple_w_gate, rel_bias_table, attn_w_qkv, attn_w_o, s5_w_in, s5_lambda_re, s5_lambda_im, s5_log_dt, s5_b_re, s5_b_im, s5_c_re, s5_c_im, s5_d, s5_w_glu, s5_b_glu, s5_w_out):
    depth = ln_g.shape[0]
    alpha = (2 * depth) ** 0.25
    bsz, seq, d = x_prompt.shape
    db, nq, _ = x_sample.shape
    assert bsz == 1 and seq % (2 * BLOCK) == 0 and d == N_HEADS * HEAD_DIM
    past_len = page_table.shape[1] * PAGE_SIZE
    assert past_len % BLOCK == 0
    xp = x_prompt.reshape(seq, d)
    xs = x_sample.reshape(db * nq, d)
    thr = _bucket_thresholds()
    tab = rel_bias_table.reshape(-1)
    vec = lambda a: a.reshape(1, -1)
    outs = {k: [] for k in ("kp", "vp", "ks", "vs", "spr", "spi", "ssr", "ssi")}
    for i in range(depth):
        li = i // 2
        w1, w3, w2 = ffn_w1[i].astype(BF16), ffn_w3[i].astype(BF16), ffn_w2[i].astype(BF16)
        g, b = ln_g[i], ln_b[i]
        xp = ffn_ln(xp, w1[0], w3[0], w2[0], vec(g[0]), vec(b[0]), alpha)
        xs = ffn_ln(xs, w1[0], w3[0], w2[0], vec(g[0]), vec(b[0]), alpha)
        if i % 2 == 0:
            wqkv = attn_w_qkv[li].astype(BF16)
            wq, wk, wv = wqkv[:, :d], wqkv[:, d:2 * d], wqkv[:, 2 * d:]
            wo = attn_w_o[li].astype(BF16)
            q, kto, vto, kta, vb, km = qkv_prompt(xp, wq, wv, wk.T, wv.T)
            kmt = jnp.transpose(km, (1, 0, 2)).reshape(d, seq // BLOCK)
            o = moba_prompt(q, kta, vb, kmt, thr, tab)
            xp = lin_ln(xp, o, wo, vec(g[1]), vec(b[1]), alpha)
            outs["kp"].append(jnp.swapaxes(kto.reshape(1, N_HEADS, HEAD_DIM, seq), 2, 3))
            outs["vp"].append(jnp.swapaxes(vto.reshape(1, N_HEADS, HEAD_DIM, seq), 2, 3))

            qkv_s = linear(xs, wqkv)
            qs, ks, vs = (_heads(t, db, nq) for t in jnp.split(qkv_s, 3, axis=-1))
            ckt, cvt = jnp.swapaxes(cache_k, 3, 4), jnp.swapaxes(cache_v, 3, 4)
            qt, knt, vnt = (jnp.swapaxes(t, 2, 3) for t in (qs, ks, vs))
            idx = sample_gate(qt, ckt, page_table, li)
            o_s = sample_attn(qt, knt, vnt, ckt, cvt, page_table, idx, thr, tab, li, past_len)
            xs = lin_ln(xs, o_s.transpose(0, 3, 1, 2).reshape(db * nq, d), wo, vec(g[1]), vec(b[1]), alpha)
            outs["ks"].append(ks)
            outs["vs"].append(vs)
        else:
            w = s5_weights(s5_w_in[li], s5_lambda_re[li], s5_lambda_im[li], s5_log_dt[li], s5_b_re[li], s5_b_im[li],
                           s5_c_re[li], s5_c_im[li], s5_d[li], s5_w_glu[li], s5_b_glu[li], s5_w_out[li])
            n_state = w["a_re"].shape[1]
            zero = jnp.zeros((1, n_state), F32)
            xp, hr, hi = s5_layer(xp, zero, zero, w, vec(g[1]), vec(b[1]), alpha, nseg=8, lseg=64, cw=1024, chained=True)
            outs["spr"].append(hr.reshape(1, -1, STATE_N))
            outs["spi"].append(hi.reshape(1, -1, STATE_N))
            xs, hr, hi = s5_layer(xs, state_s5_re[li].reshape(db, n_state), state_s5_im[li].reshape(db, n_state), w,
                                  vec(g[1]), vec(b[1]), alpha, nseg=db, lseg=nq, cw=256, chained=False)
            outs["ssr"].append(hr.reshape(db, -1, STATE_N))
            outs["ssi"].append(hi.reshape(db, -1, STATE_N))
        ple_w = (ple_w_gate[i].astype(BF16), ple_w_proj[i].astype(BF16))
        xp = ffn_ln(xp, w1[1], w3[1], w2[1], vec(g[2]), vec(b[2]), alpha, ple=(p_prompt[i].reshape(seq, -1),) + ple_w)
        xs = ffn_ln(xs, w1[1], w3[1], w2[1], vec(g[2]), vec(b[2]), alpha, ple=(p_sample[i].reshape(db * nq, -1),) + ple_w)
    st = lambda k: jnp.stack(outs[k])
    return (xp.reshape(bsz, seq, d), xs.reshape(db, nq, d), st("kp"), st("vp"), st("ks"), st("vs"),
            st("spr"), st("spi"), st("ssr"), st("ssi"))
```

```python
import functools
import math

import jax
import jax.numpy as jnp
from jax import lax
from jax.experimental import pallas as pl
from jax.experimental.pallas import tpu as pltpu

F32 = jnp.float32
BF16 = jnp.bfloat16
I32 = jnp.int32

N_HEADS = 16
HEAD_DIM = 64
BLOCK = 256
TOPK = 3
PAGE_SIZE = 128
N_BUCKETS = 32
MAX_EXACT = N_BUCKETS // 2
MAX_DISTANCE = 4096
GROUP_P = 16
STATE_N = 64
LN_EPS = 1e-5
NEG_INF = -1e30
MASK_BIG = 1e30
QK_SCALE = HEAD_DIM ** -0.5
LOG2E = math.log2(math.e)

LANES = 128
VMEM_LIMIT = 56 * 1024 * 1024

N_NEAR = MAX_DISTANCE // BLOCK + 1
Q_TILE = 2 * BLOCK
N_FAR_PAIR = (N_NEAR + 2) // 2
S5_SLAB = 16


def _params(*sem, flags=None):
    return pltpu.CompilerParams(dimension_semantics=sem, vmem_limit_bytes=VMEM_LIMIT, flags=flags)


def _resident(shape):
    nd = len(shape)
    return pl.BlockSpec(shape, lambda *_: (0,) * nd, pipeline_mode=pl.Buffered(1))


def _dot(a, b):
    return jnp.dot(a, b, preferred_element_type=F32)


def _layer_norm(y, g, b):
    mu = jnp.mean(y, axis=-1, keepdims=True)
    yc = y - mu
    var = jnp.mean(yc * yc, axis=-1, keepdims=True)
    return yc * lax.rsqrt(var + LN_EPS) * g + b


def _top3(work, cols, axis):
    out = []
    cols = cols.astype(F32)
    for _ in range(TOPK):
        mx = jnp.max(work, axis=axis, keepdims=True)
        am = jnp.min(jnp.where(work == mx, cols, jnp.inf), axis=axis, keepdims=True)
        out.append(jnp.where(mx > -jnp.inf, am, -1.0).astype(I32))
        work = jnp.where(cols == am, -jnp.inf, work)
    return out


def _bias_of(dist, head, thr_ref, tab_ref, unroll=False):
    def step(b, acc):
        return jnp.where(dist >= thr_ref[b], tab_ref[b * N_HEADS + head], acc)
    acc = jnp.full(dist.shape, tab_ref[head], F32)
    if unroll:
        for b in range(1, N_BUCKETS):
            acc = step(b, acc)
        return acc
    return lax.fori_loop(1, N_BUCKETS, step, acc)


def _ffn_body(x_ref, w1_ref, w3_ref, w2_ref, g_ref, b_ref, *rest, alpha, tf, ple):
    if ple:
        p_ref, wg_ref, wp_ref, o_ref = rest
    else:
        (o_ref,) = rest
    x = x_ref[...]
    xb = x.astype(BF16)
    acc = jnp.zeros(x.shape, F32)
    for j in range(w1_ref.shape[1] // tf):
        sl = slice(j * tf, (j + 1) * tf)
        h = jax.nn.silu(_dot(xb, w1_ref[:, sl])) * _dot(xb, w3_ref[:, sl])
        acc = acc + _dot(h.astype(BF16), w2_ref[sl, :])
    y = _layer_norm(alpha * x + 0.5 * acc, g_ref[...], b_ref[...])
    if ple:
        gate = jax.nn.sigmoid(_dot(y.astype(BF16), wg_ref[...]))
        y = y + gate * _dot(p_ref[...].astype(BF16), wp_ref[...])
    o_ref[...] = y


def ffn_ln(x, w1, w3, w2, g, b, alpha, ple=None):
    m, d = x.shape
    tm = min(512, m)
    row = lambda i: (i, 0)
    in_specs = [pl.BlockSpec((tm, d), row), _resident(w1.shape), _resident(w3.shape), _resident(w2.shape),
                _resident(g.shape), _resident(b.shape)]
    args = [x, w1, w3, w2, g, b]
    if ple is not None:
        p, wg, wp = ple
        in_specs += [pl.BlockSpec((tm, p.shape[1]), row), _resident(wg.shape), _resident(wp.shape)]
        args += [p, wg, wp]
    return pl.pallas_call(
        functools.partial(_ffn_body, alpha=alpha, tf=2 * LANES, ple=ple is not None),
        grid=(m // tm,), in_specs=in_specs, out_specs=pl.BlockSpec((tm, d), row),
        out_shape=jax.ShapeDtypeStruct((m, d), F32), compiler_params=_params("parallel"),
        name="ffn_ln")(*args)


def _lin_ln_body(x_ref, a_ref, w_ref, g_ref, b_ref, o_ref, *, alpha):
    m = _dot(a_ref[...].astype(BF16), w_ref[...])
    o_ref[...] = _layer_norm(alpha * x_ref[...] + m, g_ref[...], b_ref[...])


def lin_ln(x, a, w, g, b, alpha):
    m, d = x.shape
    tm = min(512, m)
    row = lambda i: (i, 0)
    return pl.pallas_call(
        functools.partial(_lin_ln_body, alpha=alpha),
        grid=(m // tm,),
        in_specs=[pl.BlockSpec((tm, d), row), pl.BlockSpec((tm, a.shape[1]), row), _resident(w.shape),
                  _resident(g.shape), _resident(b.shape)],
        out_specs=pl.BlockSpec((tm, d), row), out_shape=jax.ShapeDtypeStruct((m, d), F32),
        compiler_params=_params("parallel"), name="lin_ln")(x, a, w, g, b)


def _linear_body(x_ref, w_ref, o_ref):
    o_ref[...] = _dot(x_ref[...].astype(BF16), w_ref[...])


def linear(x, w):
    m = x.shape[0]
    return pl.pallas_call(
        _linear_body, grid=(1,), in_specs=[_resident(x.shape), _resident(w.shape)],
        out_specs=pl.BlockSpec((m, w.shape[1]), lambda i: (0, 0)),
        out_shape=jax.ShapeDtypeStruct((m, w.shape[1]), F32), compiler_params=_params("arbitrary"),
        name="linear")(x, w)


def _qkv_body(x_ref, wq_ref, wv_ref, wkt_ref, wvt_ref, q_ref, kto_ref, vto_ref, kta_ref, vb_ref, km_ref):
    xb = x_ref[...].astype(BF16)
    tm, d = xb.shape
    nblk = tm // BLOCK
    q_ref[...] = _dot(xb, wq_ref[...])
    vb_ref[...] = _dot(xb, wv_ref[...]).astype(BF16)
    nt = lambda w_ref: lax.dot_general(w_ref[...], xb, (((1,), (1,)), ((), ())), preferred_element_type=F32)
    kt = nt(wkt_ref)
    kto_ref[...] = kt
    vto_ref[...] = nt(wvt_ref)
    ktb = kt.astype(BF16)
    row = lax.broadcasted_iota(I32, (LANES, BLOCK), 0)
    for bk in range(nblk):
        n = pl.program_id(0) * nblk + bk
        cols = slice(bk * BLOCK, (bk + 1) * BLOCK)
        km_ref[0, :, bk:bk + 1] = jnp.sum(kt[:, cols], axis=1, keepdims=True) * (1.0 / BLOCK)
        mask_rows = jnp.where(row % HEAD_DIM == n, MASK_BIG, 0.0).astype(BF16)
        for p in range(d // LANES):
            pair = ktb[p * LANES:(p + 1) * LANES, cols]
            kta_ref[2 * p, bk] = jnp.where(row < HEAD_DIM, pair, mask_rows)
            kta_ref[2 * p + 1, bk] = jnp.where(row >= HEAD_DIM, pair, mask_rows)


def qkv_prompt(x, wq, wv, wkt, wvt):
    s, d = x.shape
    tm = 2 * BLOCK
    nb = s // BLOCK
    assert nb <= HEAD_DIM
    row = lambda i: (i, 0)
    col = lambda i: (0, i)
    return pl.pallas_call(
        _qkv_body, grid=(s // tm,),
        in_specs=[pl.BlockSpec((tm, d), row)] + [_resident(w.shape) for w in (wq, wv, wkt, wvt)],
        out_specs=[pl.BlockSpec((tm, d), row),
                   pl.BlockSpec((d, tm), col),
                   pl.BlockSpec((d, tm), col),
                   pl.BlockSpec((N_HEADS, tm // BLOCK, LANES, BLOCK), lambda i: (0, i, 0, 0)),
                   pl.BlockSpec((tm, d), row),
                   pl.BlockSpec((1, d, tm // BLOCK), lambda i: (i, 0, 0))],
        out_shape=[jax.ShapeDtypeStruct((s, d), F32),
                   jax.ShapeDtypeStruct((d, s), F32),
                   jax.ShapeDtypeStruct((d, s), F32),
                   jax.ShapeDtypeStruct((N_HEADS, nb, LANES, BLOCK), BF16),
                   jax.ShapeDtypeStruct((s, d), BF16),
                   jax.ShapeDtypeStruct((s // tm, d, tm // BLOCK), F32)],
        compiler_params=_params("parallel"), name="qkv_prompt")(x, wq, wv, wkt, wvt)


def _split_bf16(x):
    hi = x.astype(BF16)
    return hi, (x - hi.astype(F32)).astype(BF16)


def _moba_body(thr_ref, tab_ref, q_ref, kt_ref, v_ref, km_ref, o_ref, bias_ref, s_ref, p_ref):
    hp = pl.program_id(0)
    it = pl.program_id(1)
    nb = km_ref.shape[1]
    rr = lax.broadcasted_iota(I32, (BLOCK, BLOCK), 0)
    cc = lax.broadcasted_iota(I32, (BLOCK, BLOCK), 1)

    @pl.when(it == 0)
    def _():
        for hh in range(2):
            head = hp * 2 + hh

            def fill(dl, carry):
                bias_ref[hh, dl] = _bias_of(rr - cc + dl * BLOCK, head, thr_ref, tab_ref) * LOG2E
                return carry
            lax.fori_loop(0, N_NEAR, fill, 0)
            bias_ref[hh, N_NEAR] = jnp.full((BLOCK, BLOCK), tab_ref[(N_BUCKETS - 1) * N_HEADS + head] * LOG2E, F32)

    lane = lax.broadcasted_iota(I32, (Q_TILE, LANES), 1)
    own = 2 * it + lax.broadcasted_iota(I32, (Q_TILE, 1), 0) // BLOCK
    bcol = lax.broadcasted_iota(I32, (Q_TILE, nb), 1)
    causal = rr >= cc
    q = q_ref[...]
    km_hi, km_lo = _split_bf16(km_ref[...])

    def keys(hh, blk):
        return kt_ref[hh, blk]

    vlane = lax.broadcasted_iota(I32, (Q_TILE, LANES), 1)

    def attend(hh, p, pair):
        v = v_ref[pl.ds(pl.multiple_of(pair * Q_TILE, Q_TILE), Q_TILE), :]
        mine = (vlane >= hh * HEAD_DIM) & (vlane < (hh + 1) * HEAD_DIM)
        return _dot(p, jnp.where(mine, v, jnp.ones_like(v)))

    lhs, state, c_far = [], [], []
    for hh in range(2):
        in_q = (lane >= hh * HEAD_DIM) & (lane < (hh + 1) * HEAD_DIM)
        qh = jnp.where(in_q, q, 0.0)
        q_hi, q_lo = _split_bf16(qh)
        gate = _dot(q_hi, km_hi) + _dot(q_hi, km_lo) + _dot(q_lo, km_hi)
        i1, i2, i3 = _top3(jnp.where(bcol < own, gate, -jnp.inf), bcol, 1)
        e = lane - (1 - hh) * HEAD_DIM
        unsel = jnp.where((e == i1) | (e == i2) | (e == i3), 0.0, -1.0)
        qs = qh * (QK_SCALE * LOG2E)
        lhs.append(jnp.where(in_q, qs, unsel).astype(BF16))
        c_far.append(tab_ref[(N_BUCKETS - 1) * N_HEADS + hp * 2 + hh] * LOG2E)

        lhs_own = jnp.where(in_q, qs, jnp.where(e == own, 0.0, unsel)).astype(BF16)
        ss = []
        for b in range(2):
            s = _dot(lhs_own, keys(hh, 2 * it + b))
            top = s[:BLOCK] + bias_ref[hh, 0]
            bot = s[BLOCK:] + bias_ref[hh, 1 - b]
            if b == 0:
                top = jnp.where(causal, top, NEG_INF)
            else:
                bot = jnp.where(causal, bot, NEG_INF)
            ss.append(jnp.concatenate([top, bot], axis=0))
        m = jnp.max(jnp.maximum(ss[0], ss[1]), axis=-1, keepdims=True)
        p = jnp.concatenate([jnp.exp2(s - m).astype(BF16) for s in ss], axis=1)
        state += [m, attend(hh, p, it)]

    def step(jp, carry, near, slot=0):
        for hh in range(2):
            for b in range(2):
                s_ref[slot, hh, :, b * BLOCK:(b + 1) * BLOCK] = _dot(lhs[hh], keys(hh, 2 * jp + b))
        out = []
        for hh in range(2):
            m, acc = carry[2 * hh:2 * hh + 2]

            def scores(halves):
                s = (jnp.concatenate([s_ref[slot, hh, :, :BLOCK], s_ref[slot, hh, :, BLOCK:]], axis=1) if halves
                     else s_ref[slot, hh])
                if near:
                    dl = 2 * (it - jp)
                    tile = lambda a, b: bias_ref[hh, jnp.minimum(dl + a - b, N_NEAR)]
                    s = s + jnp.concatenate([jnp.concatenate([tile(0, 0), tile(0, 1)], axis=1),
                                             jnp.concatenate([tile(1, 0), tile(1, 1)], axis=1)], axis=0)
                return s

            rmax = jnp.max(scores(False), axis=-1, keepdims=True)
            if near:
                m_new = jnp.maximum(m, rmax)
                shift = m_new
            else:
                m_new = jnp.maximum(m, rmax + c_far[hh])
                shift = m_new - c_far[hh]
            p_ref[slot, hh] = jnp.exp2(scores(True) - shift).astype(BF16)
            out += [m_new, jnp.exp2(m - m_new) * acc + attend(hh, p_ref[slot, hh], jp)]
        return tuple(out)

    def run(lo, hi, carry, near):
        def two(t, c):
            return step(lo + 2 * t + 1, step(lo + 2 * t, c, near, 0), near, 1)
        n2 = (hi - lo) // 2
        carry = lax.fori_loop(0, n2, two, carry)
        return lax.fori_loop(lo + 2 * n2, hi, functools.partial(step, near=near), carry)

    n_far = jnp.maximum(it - (N_FAR_PAIR - 1), 0)
    state = run(0, n_far, tuple(state), False)
    state = run(n_far, it, state, True)
    l0 = state[1][:, HEAD_DIM:HEAD_DIM + 1]
    l1 = state[3][:, 0:1]
    o_ref[...] = jnp.where(lane < HEAD_DIM, state[1] / l0, state[3] / l1).astype(BF16)


def moba_prompt(q, kta, vb, kmt, thr, tab):
    s, d = q.shape
    nb = s // BLOCK
    smem = pl.BlockSpec(memory_space=pltpu.SMEM)
    return pl.pallas_call(
        _moba_body, grid=(d // LANES, s // Q_TILE),
        in_specs=[smem, smem,
                  pl.BlockSpec((Q_TILE, LANES), lambda hp, i: (i, hp)),
                  pl.BlockSpec((2, nb, LANES, BLOCK), lambda hp, i: (hp, 0, 0, 0)),
                  pl.BlockSpec((s, LANES), lambda hp, i: (0, hp)),
                  pl.BlockSpec((LANES, nb), lambda hp, i: (hp, 0))],
        out_specs=pl.BlockSpec((Q_TILE, LANES), lambda hp, i: (i, hp)),
        out_shape=jax.ShapeDtypeStruct((s, d), BF16),
        scratch_shapes=[pltpu.VMEM((2, N_NEAR + 1, BLOCK, BLOCK), F32), pltpu.VMEM((2, 2, Q_TILE, Q_TILE), F32),
                        pltpu.VMEM((2, 2, Q_TILE, Q_TILE), BF16)],
        compiler_params=_params("arbitrary", "arbitrary"),
        name="moba_prompt")(thr, tab, q, kta, vb, kmt)


def _s5_body(x_ref, win_ref, bbr_ref, bbi_ref, cr_ref, ci_ref, ar_ref, ai_ref, d_ref, wglu_ref, bglu_ref,
             wout_ref, g_ref, b_ref, h0r_ref, h0i_ref, perm_ref, permt_ref, o_ref, hr_ref, hi_ref,
             u_ref, y_ref, sr_ref, si_ref, *chain_scratch, alpha, nseg, lseg, cw, chained):
    step = pl.program_id(0)
    slab = S5_SLAB * GROUP_P
    sw = S5_SLAB * STATE_N
    n_slab = win_ref.shape[1] // slab
    rows = lambda j: pl.ds(pl.multiple_of(j * nseg, 8), nseg)

    if chained:
        pwr_ref, pwi_ref, cyr_ref, cyi_ref = chain_scratch

        @pl.when(step == 0)
        def _():
            cyr_ref[...] = h0r_ref[...]
            cyi_ref[...] = h0i_ref[...]
            for c in range(ar_ref.shape[1] // sw):
                cs = slice(c * sw, (c + 1) * sw)
                ar, ai = ar_ref[:, cs], ai_ref[:, cs]

                def power(j, carry):
                    pr, pi = carry
                    pwr_ref[pl.ds(j, 1), cs] = pr
                    pwi_ref[pl.ds(j, 1), cs] = pi
                    return pr * ar - pi * ai, pr * ai + pi * ar
                lax.fori_loop(0, lseg, power, (ar, ai))

    x = x_ref[...]
    xb = _dot(perm_ref[...], x.astype(BF16)).astype(BF16)
    u_ref[...] = _dot(xb, win_ref[...])
    for s in range(n_slab):
        ub = u_ref[:, s * slab:(s + 1) * slab].astype(BF16)
        sr_ref[...] = _dot(ub, bbr_ref[s])
        si_ref[...] = _dot(ub, bbi_ref[s])
        for c in range(sw // cw):
            cs = slice(c * cw, (c + 1) * cw)
            gs = slice(s * sw + c * cw, s * sw + (c + 1) * cw)
            ar = jnp.broadcast_to(ar_ref[:, gs], (nseg, cw))
            ai = jnp.broadcast_to(ai_ref[:, gs], (nseg, cw))

            def scan(j, carry):
                hr, hi = carry
                nr = ar * hr - ai * hi + sr_ref[rows(j), cs]
                ni = ar * hi + ai * hr + si_ref[rows(j), cs]
                sr_ref[rows(j), cs] = nr
                si_ref[rows(j), cs] = ni
                return nr, ni

            if not chained:
                er, ei = lax.fori_loop(0, lseg, scan, (h0r_ref[:, gs], h0i_ref[:, gs]))
                hr_ref[:, gs] = er
                hi_ref[:, gs] = ei
            else:
                zero = jnp.zeros((nseg, cw), F32)
                er, ei = lax.fori_loop(0, lseg, scan, (zero, zero))
                plr, pli = pwr_ref[lseg - 1:lseg, gs], pwi_ref[lseg - 1:lseg, gs]
                cr, ci = cyr_ref[:, gs], cyi_ref[:, gs]
                hsr, hsi = [], []
                for r in range(nseg):
                    hsr.append(cr)
                    hsi.append(ci)
                    cr, ci = (er[r:r + 1] + plr * cr - pli * ci, ei[r:r + 1] + plr * ci + pli * cr)
                cyr_ref[:, gs] = cr
                cyi_ref[:, gs] = ci
                hsr = jnp.concatenate(hsr, axis=0)
                hsi = jnp.concatenate(hsi, axis=0)

                def fix(j, carry):
                    pr = jnp.broadcast_to(pwr_ref[pl.ds(j, 1), gs], (nseg, cw))
                    pi = jnp.broadcast_to(pwi_ref[pl.ds(j, 1), gs], (nseg, cw))
                    sr_ref[rows(j), cs] = sr_ref[rows(j), cs] + (pr * hsr - pi * hsi)
                    si_ref[rows(j), cs] = si_ref[rows(j), cs] + (pr * hsi + pi * hsr)
                    return carry
                lax.fori_loop(0, lseg, fix, 0)
        ys = _dot(sr_ref[...].astype(BF16), cr_ref[s]) - _dot(si_ref[...].astype(BF16), ci_ref[s])
        us = slice(s * slab, (s + 1) * slab)
        y_ref[:, us] = ys + d_ref[:, us] * u_ref[:, us]
    if chained:
        hr_ref[...] = cyr_ref[...]
        hi_ref[...] = cyi_ref[...]
    y = y_ref[...]
    z = jax.nn.gelu(y) * jax.nn.sigmoid(_dot(y.astype(BF16), wglu_ref[...]) + bglu_ref[...])
    zb = _dot(permt_ref[...], z.astype(BF16)).astype(BF16)
    m = _dot(zb, wout_ref[...])
    o_ref[...] = _layer_norm(alpha * x + m, g_ref[...], b_ref[...])


def s5_layer(x, h0r, h0i, w, g, b, alpha, *, nseg, lseg, cw, chained):
    m, d = x.shape
    tb = nseg * lseg
    n_state = h0r.shape[1]
    sw = S5_SLAB * STATE_N
    row = lambda i: (i, 0)
    hs = h0r.shape
    nat = jnp.arange(tb, dtype=I32)
    perm = ((nat[:, None] % nseg) * lseg + nat[:, None] // nseg == nat[None, :]).astype(BF16)
    scratch = [pltpu.VMEM((tb, d), F32), pltpu.VMEM((tb, d), F32), pltpu.VMEM((tb, sw), F32), pltpu.VMEM((tb, sw), F32)]
    if chained:
        scratch += [pltpu.VMEM((lseg, n_state), F32), pltpu.VMEM((lseg, n_state), F32),
                    pltpu.VMEM((1, n_state), F32), pltpu.VMEM((1, n_state), F32)]
    else:
        assert m == tb
    weights = [w["w_in"], w["bb_re"], w["bb_im"], w["c_re"], w["c_im"], w["a_re"], w["a_im"], w["d"],
               w["w_glu"], w["b_glu"], w["w_out"], g, b, h0r, h0i, perm, perm.T]
    return pl.pallas_call(
        functools.partial(_s5_body, alpha=alpha, nseg=nseg, lseg=lseg, cw=cw, chained=chained),
        grid=(m // tb,),
        in_specs=[pl.BlockSpec((tb, d), row)] + [_resident(a.shape) for a in weights],
        out_specs=[pl.BlockSpec((tb, d), row), pl.BlockSpec(hs, lambda i: (0, 0)), pl.BlockSpec(hs, lambda i: (0, 0))],
        out_shape=[jax.ShapeDtypeStruct((m, d), F32), jax.ShapeDtypeStruct(hs, F32), jax.ShapeDtypeStruct(hs, F32)],
        scratch_shapes=scratch, compiler_params=_params("arbitrary"), name="s5_layer")(x, *weights)


def s5_weights(w_in, lam_re, lam_im, log_dt, b_re, b_im, c_re, c_im, d, w_glu, b_glu, w_out):
    dt = jnp.exp(log_dt)[:, None]
    mag = jnp.exp(lam_re * dt)
    ang = lam_im * dt
    ab_re, ab_im = mag * jnp.cos(ang), mag * jnp.sin(ang)
    den = lam_re * lam_re + lam_im * lam_im
    f_re = ((ab_re - 1.0) * lam_re + ab_im * lam_im) / den
    f_im = (ab_im * lam_re - (ab_re - 1.0) * lam_im) / den
    bb_re = f_re[..., None] * b_re - f_im[..., None] * b_im
    bb_im = f_re[..., None] * b_im + f_im[..., None] * b_re
    n_g = lam_re.shape[0]
    eye = jnp.eye(S5_SLAB, dtype=F32)

    def in_slabs(bb):
        t = jnp.transpose(bb, (0, 2, 1)).reshape(n_g // S5_SLAB, S5_SLAB, GROUP_P, STATE_N)
        return jnp.einsum("sgpn,gh->sgphn", t, eye).reshape(n_g // S5_SLAB, S5_SLAB * GROUP_P, S5_SLAB * STATE_N).astype(BF16)

    def out_slabs(c):
        t = jnp.transpose(c, (0, 2, 1)).reshape(n_g // S5_SLAB, S5_SLAB, STATE_N, GROUP_P)
        return jnp.einsum("sgnp,gh->sgnhp", t, eye).reshape(n_g // S5_SLAB, S5_SLAB * STATE_N, S5_SLAB * GROUP_P).astype(BF16)

    return dict(w_in=w_in.astype(BF16), bb_re=in_slabs(bb_re), bb_im=in_slabs(bb_im), c_re=out_slabs(c_re),
                c_im=out_slabs(c_im), a_re=ab_re.reshape(1, -1), a_im=ab_im.reshape(1, -1), d=d.reshape(1, -1),
                w_glu=w_glu.astype(BF16), b_glu=b_glu.reshape(1, -1), w_out=w_out.astype(BF16))


def _sgate_body(pt_ref, q_ref, *rest, n_pg, nb):
    pages, idx_ref, km_ref = rest[:n_pg], rest[n_pg], rest[n_pg + 1]
    pc = pl.program_id(1)
    ppb = BLOCK // PAGE_SIZE
    lane = lax.broadcasted_iota(I32, km_ref.shape, 2)

    @pl.when(pc == 0)
    def _():
        km_ref[...] = jnp.zeros(km_ref.shape, F32)

    km = km_ref[...]
    for u in range(n_pg // ppb):
        tot = functools.reduce(jnp.add, [pages[u * ppb + t][0, 0] for t in range(ppb)])
        mean = jnp.sum(tot, axis=-1, keepdims=True) * (1.0 / BLOCK)
        km = jnp.where(lane == pc * (n_pg // ppb) + u, mean, km)
    km_ref[...] = km

    @pl.when(pc == pl.num_programs(1) - 1)
    def _():
        km = km_ref[...]
        bl = lax.broadcasted_iota(I32, (km.shape[0], km.shape[2]), 1)
        for qi in range(q_ref.shape[3]):
            gate = jnp.sum(km * q_ref[0, :, :, qi:qi + 1], axis=1)
            for t, am in enumerate(_top3(jnp.where(bl < nb, gate, -jnp.inf), bl, 1)):
                idx_ref[0, qi, :, t:t + 1] = am


def sample_gate(qt, cache_kt, page_table, li):
    db, nh, dh, nq = qt.shape
    n_pages = page_table.shape[1]
    nb = n_pages * PAGE_SIZE // BLOCK
    assert nb <= LANES
    n_pg = 8
    page = lambda t: pl.BlockSpec((1, 1, nh, dh, PAGE_SIZE),
                                  lambda b, pc, pt: (li, pt[b * n_pages + pc * n_pg + t], 0, 0, 0))
    return pl.pallas_call(
        functools.partial(_sgate_body, n_pg=n_pg, nb=nb),
        grid_spec=pltpu.PrefetchScalarGridSpec(
            num_scalar_prefetch=1, grid=(db, n_pages // n_pg),
            in_specs=[pl.BlockSpec((1, nh, dh, nq), lambda b, pc, pt: (b, 0, 0, 0))] + [page(t) for t in range(n_pg)],
            out_specs=pl.BlockSpec((1, nq, nh, TOPK), lambda b, pc, pt: (b, 0, 0, 0)),
            scratch_shapes=[pltpu.VMEM((nh, dh, LANES), F32)]),
        out_shape=jax.ShapeDtypeStruct((db, nq, nh, TOPK), I32),
        compiler_params=_params("arbitrary", "arbitrary"), name="sample_gate")(
            page_table.reshape(-1), qt, *([cache_kt] * n_pg))


def _sattn_body(pt_ref, idx_ref, thr_ref, tab_ref, q_ref, kn_ref, vn_ref, ck_hbm, cv_hbm, o_ref,
                kbuf, vbuf, sem, *, nq, nh, n_pages, li, past_len):
    ppb = BLOCK // PAGE_SIZE
    b = pl.program_id(0)
    far = thr_ref[N_BUCKETS - 1]
    off = lax.broadcasted_iota(I32, (1, BLOCK), 1)
    d_new = lax.broadcasted_iota(I32, (nq, nq), 0) - lax.broadcasted_iota(I32, (nq, nq), 1)

    def page_copies(bb, h, slot, fetch):
        out = []
        for qi in range(nq):
            for t in range(TOPK):
                blk = idx_ref[((bb * nq + qi) * nh + h) * TOPK + t] if fetch else 0
                for pp in range(ppb):
                    page = pt_ref[bb * n_pages + blk * ppb + pp] if fetch else 0
                    j = (qi * TOPK + t) * ppb + pp
                    out.append(pltpu.make_async_copy(ck_hbm.at[li, page, h], kbuf.at[slot, j], sem.at[slot]))
                    out.append(pltpu.make_async_copy(cv_hbm.at[li, page, h], vbuf.at[slot, j], sem.at[slot]))
        return out

    @pl.when(b == 0)
    def _():
        for c in page_copies(0, 0, 0, True):
            c.start()

    def head(h, carry):
        slot = h % 2
        last_head = h == nh - 1
        nb_, nh_ = jnp.where(last_head, b + 1, b), jnp.where(last_head, 0, h + 1)

        @pl.when(nb_ < pl.num_programs(0))
        def _():
            for c in page_copies(nb_, nh_, 1 - slot, True):
                c.start()

        for c in page_copies(0, 0, slot, False):
            c.wait()

        last = tab_ref[(N_BUCKETS - 1) * N_HEADS + h]
        kn, vn = kn_ref[0, h], vn_ref[0, h]
        bias_new = _bias_of(d_new, h, thr_ref, tab_ref, unroll=True)
        for qi in range(nq):
            qcol = q_ref[0, h, :, qi:qi + 1] * QK_SCALE
            scores = []
            for t in range(TOPK):
                blk = idx_ref[((b * nq + qi) * nh + h) * TOPK + t]
                dist = (past_len + qi) - (blk * BLOCK + off)
                bias = lax.cond((past_len + qi) - (blk * BLOCK + BLOCK - 1) >= far,
                                lambda: jnp.full((1, BLOCK), last, F32),
                                lambda dist=dist: _bias_of(dist, h, thr_ref, tab_ref, unroll=True))
                for pp in range(ppb):
                    kt = kbuf[slot, (qi * TOPK + t) * ppb + pp]
                    scores.append(jnp.sum(kt * qcol, axis=0, keepdims=True)
                                  + bias[:, pp * PAGE_SIZE:(pp + 1) * PAGE_SIZE])
            s_own = jnp.sum(kn * qcol, axis=0, keepdims=True) + bias_new[qi:qi + 1, :]
            scores.append(jnp.where(d_new[qi:qi + 1, :] >= 0, s_own, NEG_INF))
            m = functools.reduce(jnp.maximum, [jnp.max(s, axis=1, keepdims=True) for s in scores])
            ps = [jnp.exp(s - m) for s in scores]
            l = functools.reduce(jnp.add, [jnp.sum(p, axis=1, keepdims=True) for p in ps])
            acc = functools.reduce(jnp.add, [vbuf[slot, qi * TOPK * ppb + j] * (ps[j] / l) for j in range(TOPK * ppb)])
            o = jnp.sum(acc, axis=1, keepdims=True) + jnp.sum(vn * (ps[-1] / l), axis=1, keepdims=True)
            o_ref[0, h, :, qi:qi + 1] = o
        return carry

    lax.fori_loop(0, nh, head, 0)


def sample_attn(qt, knt, vnt, cache_kt, cache_vt, page_table, idx, thr, tab, li, past_len):
    db, nh, dh, nq = qt.shape
    assert nh % 2 == 0
    n_pages = page_table.shape[1]
    n_pg = nq * TOPK * (BLOCK // PAGE_SIZE)
    tok = pl.BlockSpec((1, nh, dh, nq), lambda b, pt, ix, th: (b, 0, 0, 0))
    hbm = pl.BlockSpec(memory_space=pl.ANY)
    return pl.pallas_call(
        functools.partial(_sattn_body, nq=nq, nh=nh, n_pages=n_pages, li=li, past_len=past_len),
        grid_spec=pltpu.PrefetchScalarGridSpec(
            num_scalar_prefetch=3, grid=(db,),
            in_specs=[pl.BlockSpec(memory_space=pltpu.SMEM), tok, tok, tok, hbm, hbm],
            out_specs=tok,
            scratch_shapes=[pltpu.VMEM((2, n_pg, dh, PAGE_SIZE), F32), pltpu.VMEM((2, n_pg, dh, PAGE_SIZE), F32),
                            pltpu.SemaphoreType.DMA((2,))]),
        out_shape=jax.ShapeDtypeStruct((db, nh, dh, nq), F32),
        compiler_params=_params("arbitrary"), name="sample_attn")(
            page_table.reshape(-1), idx.reshape(-1), thr, tab, qt, knt, vnt, cache_kt, cache_vt)


def _bucket_thresholds():
    n = jnp.arange(MAX_DISTANCE + 1, dtype=I32)
    nf = jnp.maximum(n, 1).astype(F32)
    log_b = MAX_EXACT + (jnp.log(nf / MAX_EXACT) / math.log(MAX_DISTANCE / MAX_EXACT)
                         * (N_BUCKETS - MAX_EXACT)).astype(I32)
    bucket = jnp.where(n < MAX_EXACT, n, jnp.minimum(log_b, N_BUCKETS - 1))
    return jnp.sum(bucket[None, :] < jnp.arange(N_BUCKETS, dtype=I32)[:, None], axis=1).astype(I32)


def _heads(t, b, s):
    return t.reshape(b, s, N_HEADS, HEAD_DIM).transpose(0, 2, 1, 3)


def kernel(x_prompt, x_sample, cache_k, cache_v, state_s5_re, state_s5_im, page_table, p_prompt, p_sample, ln_g, ln_b, ffn_w1, ffn_w3, ffn_w2, ple_w_proj, ple_w_gate, rel_bias_table, attn_w_qkv, attn_w_o, s5_w_in, s5_lambda_re, s5_lambda_im, s5_log_dt, s5_b_re, s5_b_im, s5_c_re, s5_c_im, s5_d, s5_w_glu, s5_b_glu, s5_w_out):
    depth = ln_g.shape[0]
    alpha = (2 * depth) ** 0.25
    bsz, seq, d = x_prompt.shape
    db, nq, _ = x_sample.shape
    assert bsz == 1 and seq % (2 * BLOCK) == 0 and d == N_HEADS * HEAD_DIM
    past_len = page_table.shape[1] * PAGE_SIZE
    assert past_len % BLOCK == 0
    xp = x_prompt.reshape(seq, d)
    xs = x_sample.reshape(db * nq, d)
    thr = _bucket_thresholds()
    tab = rel_bias_table.reshape(-1)
    vec = lambda a: a.reshape(1, -1)
    outs = {k: [] for k in ("kp", "vp", "ks", "vs", "spr", "spi", "ssr", "ssi")}
    for i in range(depth):
        li = i // 2
        w1, w3, w2 = ffn_w1[i].astype(BF16), ffn_w3[i].astype(BF16), ffn_w2[i].astype(BF16)
        g, b = ln_g[i], ln_b[i]
        xp = ffn_ln(xp, w1[0], w3[0], w2[0], vec(g[0]), vec(b[0]), alpha)
        xs = ffn_ln(xs, w1[0], w3[0], w2[0], vec(g[0]), vec(b[0]), alpha)
        if i % 2 == 0:
            wqkv = attn_w_qkv[li].astype(BF16)
            wq, wk, wv = wqkv[:, :d], wqkv[:, d:2 * d], wqkv[:, 2 * d:]
            wo = attn_w_o[li].astype(BF16)
            q, kto, vto, kta, vb, km = qkv_prompt(xp, wq, wv, wk.T, wv.T)
            kmt = jnp.transpose(km, (1, 0, 2)).reshape(d, seq // BLOCK)
            o = moba_prompt(q, kta, vb, kmt, thr, tab)
            xp = lin_ln(xp, o, wo, vec(g[1]), vec(b[1]), alpha)
            outs["kp"].append(jnp.swapaxes(kto.reshape(1, N_HEADS, HEAD_DIM, seq), 2, 3))
            outs["vp"].append(jnp.swapaxes(vto.reshape(1, N_HEADS, HEAD_DIM, seq), 2, 3))

            qkv_s = linear(xs, wqkv)
            qs, ks, vs = (_heads(t, db, nq) for t in jnp.split(qkv_s, 3, axis=-1))
            ckt, cvt = jnp.swapaxes(cache_k, 3, 4), jnp.swapaxes(cache_v, 3, 4)
            qt, knt, vnt = (jnp.swapaxes(t, 2, 3) for t in (qs, ks, vs))
            idx = sample_gate(qt, ckt, page_table, li)
            o_s = sample_attn(qt, knt, vnt, ckt, cvt, page_table, idx, thr, tab, li, past_len)
            xs = lin_ln(xs, o_s.transpose(0, 3, 1, 2).reshape(db * nq, d), wo, vec(g[1]), vec(b[1]), alpha)
            outs["ks"].append(ks)
            outs["vs"].append(vs)
        else:
            w = s5_weights(s5_w_in[li], s5_lambda_re[li], s5_lambda_im[li], s5_log_dt[li], s5_b_re[li], s5_b_im[li],
                           s5_c_re[li], s5_c_im[li], s5_d[li], s5_w_glu[li], s5_b_glu[li], s5_w_out[li])
            n_state = w["a_re"].shape[1]
            zero = jnp.zeros((1, n_state), F32)
            xp, hr, hi = s5_layer(xp, zero, zero, w, vec(g[1]), vec(b[1]), alpha, nseg=8, lseg=64, cw=1024, chained=True)
            outs["spr"].append(hr.reshape(1, -1, STATE_N))
            outs["spi"].append(hi.reshape(1, -1, STATE_N))
            xs, hr, hi = s5_layer(xs, state_s5_re[li].reshape(db, n_state), state_s5_im[li].reshape(db, n_state), w,
                                  vec(g[1]), vec(b[1]), alpha, nseg=db, lseg=nq, cw=256, chained=False)
            outs["ssr"].append(hr.reshape(db, -1, STATE_N))
            outs["ssi"].append(hi.reshape(db, -1, STATE_N))
        ple_w = (ple_w_gate[i].astype(BF16), ple_w_proj[i].astype(BF16))
        xp = ffn_ln(xp, w1[1], w3[1], w2[1], vec(g[2]), vec(b[2]), alpha, ple=(p_prompt[i].reshape(seq, -1),) + ple_w)
        xs = ffn_ln(xs, w1[1], w3[1], w2[1], vec(g[2]), vec(b[2]), alpha, ple=(p_sample[i].reshape(db * nq, -1),) + ple_w)
    st = lambda k: jnp.stack(outs[k])
    return (xp.reshape(bsz, seq, d), xs.reshape(db, nq, d), st("kp"), st("vp"), st("ks"), st("vs"),
            st("spr"), st("spi"), st("ssr"), st("ssi"))
```

```python
import functools
import math

import jax
import jax.numpy as jnp
from jax import lax
from jax.experimental import pallas as pl
from jax.experimental.pallas import tpu as pltpu

F32 = jnp.float32
BF16 = jnp.bfloat16
I32 = jnp.int32

N_HEADS = 16
HEAD_DIM = 64
BLOCK = 256
TOPK = 3
PAGE_SIZE = 128
N_BUCKETS = 32
MAX_EXACT = N_BUCKETS // 2
MAX_DISTANCE = 4096
GROUP_P = 16
STATE_N = 64
LN_EPS = 1e-5
NEG_INF = -1e30
MASK_BIG = 1e30
QK_SCALE = HEAD_DIM ** -0.5
LOG2E = math.log2(math.e)

LANES = 128
VMEM_LIMIT = 56 * 1024 * 1024

N_NEAR = MAX_DISTANCE // BLOCK + 1
Q_TILE = 2 * BLOCK
N_FAR_PAIR = (N_NEAR + 2) // 2
S5_SLAB = 16


def _params(*sem, flags=None):
    return pltpu.CompilerParams(dimension_semantics=sem, vmem_limit_bytes=VMEM_LIMIT, flags=flags)


def _resident(shape):
    nd = len(shape)
    return pl.BlockSpec(shape, lambda *_: (0,) * nd, pipeline_mode=pl.Buffered(1))


def _dot(a, b):
    return jnp.dot(a, b, preferred_element_type=F32)


def _layer_norm(y, g, b):
    mu = jnp.mean(y, axis=-1, keepdims=True)
    yc = y - mu
    var = jnp.mean(yc * yc, axis=-1, keepdims=True)
    return yc * lax.rsqrt(var + LN_EPS) * g + b


def _top3(work, cols, axis):
    out = []
    cols = cols.astype(F32)
    for _ in range(TOPK):
        mx = jnp.max(work, axis=axis, keepdims=True)
        am = jnp.min(jnp.where(work == mx, cols, jnp.inf), axis=axis, keepdims=True)
        out.append(jnp.where(mx > -jnp.inf, am, -1.0).astype(I32))
        work = jnp.where(cols == am, -jnp.inf, work)
    return out


def _bias_of(dist, head, thr_ref, tab_ref, unroll=False):
    def step(b, acc):
        return jnp.where(dist >= thr_ref[b], tab_ref[b * N_HEADS + head], acc)
    acc = jnp.full(dist.shape, tab_ref[head], F32)
    if unroll:
        for b in range(1, N_BUCKETS):
            acc = step(b, acc)
        return acc
    return lax.fori_loop(1, N_BUCKETS, step, acc)


def _ffn_body(x_ref, w1_ref, w3_ref, w2_ref, g_ref, b_ref, *rest, alpha, tf, ple):
    if ple:
        p_ref, wg_ref, wp_ref, o_ref = rest
    else:
        (o_ref,) = rest
    x = x_ref[...]
    xb = x.astype(BF16)
    acc = jnp.zeros(x.shape, F32)
    for j in range(w1_ref.shape[1] // tf):
        sl = slice(j * tf, (j + 1) * tf)
        h = jax.nn.silu(_dot(xb, w1_ref[:, sl])) * _dot(xb, w3_ref[:, sl])
        acc = acc + _dot(h.astype(BF16), w2_ref[sl, :])
    y = _layer_norm(alpha * x + 0.5 * acc, g_ref[...], b_ref[...])
    if ple:
        gate = jax.nn.sigmoid(_dot(y.astype(BF16), wg_ref[...]))
        y = y + gate * _dot(p_ref[...].astype(BF16), wp_ref[...])
    o_ref[...] = y


def ffn_ln(x, w1, w3, w2, g, b, alpha, ple=None):
    m, d = x.shape
    tm = min(512, m)
    row = lambda i: (i, 0)
    in_specs = [pl.BlockSpec((tm, d), row), _resident(w1.shape), _resident(w3.shape), _resident(w2.shape),
                _resident(g.shape), _resident(b.shape)]
    args = [x, w1, w3, w2, g, b]
    if ple is not None:
        p, wg, wp = ple
        in_specs += [pl.BlockSpec((tm, p.shape[1]), row), _resident(wg.shape), _resident(wp.shape)]
        args += [p, wg, wp]
    return pl.pallas_call(
        functools.partial(_ffn_body, alpha=alpha, tf=2 * LANES, ple=ple is not None),
        grid=(m // tm,), in_specs=in_specs, out_specs=pl.BlockSpec((tm, d), row),
        out_shape=jax.ShapeDtypeStruct((m, d), F32), compiler_params=_params("parallel"),
        name="ffn_ln")(*args)


def _lin_ln_body(x_ref, a_ref, w_ref, g_ref, b_ref, o_ref, *, alpha):
    m = _dot(a_ref[...].astype(BF16), w_ref[...])
    o_ref[...] = _layer_norm(alpha * x_ref[...] + m, g_ref[...], b_ref[...])


def lin_ln(x, a, w, g, b, alpha):
    m, d = x.shape
    tm = min(512, m)
    row = lambda i: (i, 0)
    return pl.pallas_call(
        functools.partial(_lin_ln_body, alpha=alpha),
        grid=(m // tm,),
        in_specs=[pl.BlockSpec((tm, d), row), pl.BlockSpec((tm, a.shape[1]), row), _resident(w.shape),
                  _resident(g.shape), _resident(b.shape)],
        out_specs=pl.BlockSpec((tm, d), row), out_shape=jax.ShapeDtypeStruct((m, d), F32),
        compiler_params=_params("parallel"), name="lin_ln")(x, a, w, g, b)


def _linear_body(x_ref, w_ref, o_ref):
    o_ref[...] = _dot(x_ref[...].astype(BF16), w_ref[...])


def linear(x, w):
    m = x.shape[0]
    return pl.pallas_call(
        _linear_body, grid=(1,), in_specs=[_resident(x.shape), _resident(w.shape)],
        out_specs=pl.BlockSpec((m, w.shape[1]), lambda i: (0, 0)),
        out_shape=jax.ShapeDtypeStruct((m, w.shape[1]), F32), compiler_params=_params("arbitrary"),
        name="linear")(x, w)


def _qkv_body(x_ref, wq_ref, wv_ref, wkt_ref, wvt_ref, q_ref, kto_ref, vto_ref, kta_ref, vb_ref, km_ref):
    xb = x_ref[...].astype(BF16)
    tm, d = xb.shape
    nblk = tm // BLOCK
    q_ref[...] = _dot(xb, wq_ref[...])
    vb_ref[...] = _dot(xb, wv_ref[...]).astype(BF16)
    nt = lambda w_ref: lax.dot_general(w_ref[...], xb, (((1,), (1,)), ((), ())), preferred_element_type=F32)
    kt = nt(wkt_ref)
    kto_ref[...] = kt
    vto_ref[...] = nt(wvt_ref)
    ktb = kt.astype(BF16)
    row = lax.broadcasted_iota(I32, (LANES, BLOCK), 0)
    for bk in range(nblk):
        n = pl.program_id(0) * nblk + bk
        cols = slice(bk * BLOCK, (bk + 1) * BLOCK)
        km_ref[0, :, bk:bk + 1] = jnp.sum(kt[:, cols], axis=1, keepdims=True) * (1.0 / BLOCK)
        mask_rows = jnp.where(row % HEAD_DIM == n, MASK_BIG, 0.0).astype(BF16)
        for p in range(d // LANES):
            pair = ktb[p * LANES:(p + 1) * LANES, cols]
            kta_ref[2 * p, bk] = jnp.where(row < HEAD_DIM, pair, mask_rows)
            kta_ref[2 * p + 1, bk] = jnp.where(row >= HEAD_DIM, pair, mask_rows)


def qkv_prompt(x, wq, wv, wkt, wvt):
    s, d = x.shape
    tm = 2 * BLOCK
    nb = s // BLOCK
    assert nb <= HEAD_DIM
    row = lambda i: (i, 0)
    col = lambda i: (0, i)
    return pl.pallas_call(
        _qkv_body, grid=(s // tm,),
        in_specs=[pl.BlockSpec((tm, d), row)] + [_resident(w.shape) for w in (wq, wv, wkt, wvt)],
        out_specs=[pl.BlockSpec((tm, d), row),
                   pl.BlockSpec((d, tm), col),
                   pl.BlockSpec((d, tm), col),
                   pl.BlockSpec((N_HEADS, tm // BLOCK, LANES, BLOCK), lambda i: (0, i, 0, 0)),
                   pl.BlockSpec((tm, d), row),
                   pl.BlockSpec((1, d, tm // BLOCK), lambda i: (i, 0, 0))],
        out_shape=[jax.ShapeDtypeStruct((s, d), F32),
                   jax.ShapeDtypeStruct((d, s), F32),
                   jax.ShapeDtypeStruct((d, s), F32),
                   jax.ShapeDtypeStruct((N_HEADS, nb, LANES, BLOCK), BF16),
                   jax.ShapeDtypeStruct((s, d), BF16),
                   jax.ShapeDtypeStruct((s // tm, d, tm // BLOCK), F32)],
        compiler_params=_params("parallel"), name="qkv_prompt")(x, wq, wv, wkt, wvt)


def _split_bf16(x):
    hi = x.astype(BF16)
    return hi, (x - hi.astype(F32)).astype(BF16)


def _moba_body(thr_ref, tab_ref, q_ref, kt_ref, v_ref, km_ref, o_ref, bias_ref, s_ref, p_ref):
    hp = pl.program_id(0)
    it = pl.program_id(1)
    nb = km_ref.shape[1]
    rr = lax.broadcasted_iota(I32, (BLOCK, BLOCK), 0)
    cc = lax.broadcasted_iota(I32, (BLOCK, BLOCK), 1)

    @pl.when(it == 0)
    def _():
        for hh in range(2):
            head = hp * 2 + hh

            def fill(dl, carry):
                bias_ref[hh, dl] = _bias_of(rr - cc + dl * BLOCK, head, thr_ref, tab_ref) * LOG2E
                return carry
            lax.fori_loop(0, N_NEAR, fill, 0)
            bias_ref[hh, N_NEAR] = jnp.full((BLOCK, BLOCK), tab_ref[(N_BUCKETS - 1) * N_HEADS + head] * LOG2E, F32)

    lane = lax.broadcasted_iota(I32, (Q_TILE, LANES), 1)
    own = 2 * it + lax.broadcasted_iota(I32, (Q_TILE, 1), 0) // BLOCK
    bcol = lax.broadcasted_iota(I32, (Q_TILE, nb), 1)
    causal = rr >= cc
    q = q_ref[...]
    km_hi, km_lo = _split_bf16(km_ref[...])

    def keys(hh, blk):
        return kt_ref[hh, blk]

    vlane = lax.broadcasted_iota(I32, (Q_TILE, LANES), 1)

    def attend(hh, p, pair):
        v = v_ref[pl.ds(pl.multiple_of(pair * Q_TILE, Q_TILE), Q_TILE), :]
        mine = (vlane >= hh * HEAD_DIM) & (vlane < (hh + 1) * HEAD_DIM)
        return _dot(p, jnp.where(mine, v, jnp.ones_like(v)))

    lhs, state, c_far = [], [], []
    for hh in range(2):
        in_q = (lane >= hh * HEAD_DIM) & (lane < (hh + 1) * HEAD_DIM)
        qh = jnp.where(in_q, q, 0.0)
        q_hi, q_lo = _split_bf16(qh)
        gate = _dot(q_hi, km_hi) + _dot(q_hi, km_lo) + _dot(q_lo, km_hi)
        i1, i2, i3 = _top3(jnp.where(bcol < own, gate, -jnp.inf), bcol, 1)
        e = lane - (1 - hh) * HEAD_DIM
        unsel = jnp.where((e == i1) | (e == i2) | (e == i3), 0.0, -1.0)
        qs = qh * (QK_SCALE * LOG2E)
        lhs.append(jnp.where(in_q, qs, unsel).astype(BF16))
        c_far.append(tab_ref[(N_BUCKETS - 1) * N_HEADS + hp * 2 + hh] * LOG2E)

        lhs_own = jnp.where(in_q, qs, jnp.where(e == own, 0.0, unsel)).astype(BF16)
        ss = []
        for b in range(2):
            s = _dot(lhs_own, keys(hh, 2 * it + b))
            top = s[:BLOCK] + bias_ref[hh, 0]
            bot = s[BLOCK:] + bias_ref[hh, 1 - b]
            if b == 0:
                top = jnp.where(causal, top, NEG_INF)
            else:
                bot = jnp.where(causal, bot, NEG_INF)
            ss.append(jnp.concatenate([top, bot], axis=0))
        m = jnp.max(jnp.maximum(ss[0], ss[1]), axis=-1, keepdims=True)
        p = jnp.concatenate([jnp.exp2(s - m).astype(BF16) for s in ss], axis=1)
        state += [m, attend(hh, p, it)]

    def step(jp, carry, near, slot=0):
        for hh in range(2):
            for b in range(2):
                s_ref[slot, hh, :, b * BLOCK:(b + 1) * BLOCK] = _dot(lhs[hh], keys(hh, 2 * jp + b))
        out = []
        for hh in range(2):
            m, acc = carry[2 * hh:2 * hh + 2]

            def scores(halves):
                s = (jnp.concatenate([s_ref[slot, hh, :, :BLOCK], s_ref[slot, hh, :, BLOCK:]], axis=1) if halves
                     else s_ref[slot, hh])
                if near:
                    dl = 2 * (it - jp)
                    tile = lambda a, b: bias_ref[hh, jnp.minimum(dl + a - b, N_NEAR)]
                    s = s + jnp.concatenate([jnp.concatenate([tile(0, 0), tile(0, 1)], axis=1),
                                             jnp.concatenate([tile(1, 0), tile(1, 1)], axis=1)], axis=0)
                return s

            rmax = jnp.max(scores(False), axis=-1, keepdims=True)
            if near:
                m_new = jnp.maximum(m, rmax)
                shift = m_new
            else:
                m_new = jnp.maximum(m, rmax + c_far[hh])
                shift = m_new - c_far[hh]
            p_ref[slot, hh] = jnp.exp2(scores(True) - shift).astype(BF16)
            out += [m_new, jnp.exp2(m - m_new) * acc + attend(hh, p_ref[slot, hh], jp)]
        return tuple(out)

    def run(lo, hi, carry, near):
        def two(t, c):
            return step(lo + 2 * t + 1, step(lo + 2 * t, c, near, 0), near, 1)
        n2 = (hi - lo) // 2
        carry = lax.fori_loop(0, n2, two, carry)
        return lax.fori_loop(lo + 2 * n2, hi, functools.partial(step, near=near), carry)

    n_far = jnp.maximum(it - (N_FAR_PAIR - 1), 0)
    state = run(0, n_far, tuple(state), False)
    state = run(n_far, it, state, True)
    l0 = state[1][:, HEAD_DIM:HEAD_DIM + 1]
    l1 = state[3][:, 0:1]
    o_ref[...] = jnp.where(lane < HEAD_DIM, state[1] / l0, state[3] / l1).astype(BF16)


def moba_prompt(q, kta, vb, kmt, thr, tab):
    s, d = q.shape
    nb = s // BLOCK
    smem = pl.BlockSpec(memory_space=pltpu.SMEM)
    return pl.pallas_call(
        _moba_body, grid=(d // LANES, s // Q_TILE),
        in_specs=[smem, smem,
                  pl.BlockSpec((Q_TILE, LANES), lambda hp, i: (i, hp)),
                  pl.BlockSpec((2, nb, LANES, BLOCK), lambda hp, i: (hp, 0, 0, 0)),
                  pl.BlockSpec((s, LANES), lambda hp, i: (0, hp)),
                  pl.BlockSpec((LANES, nb), lambda hp, i: (hp, 0))],
        out_specs=pl.BlockSpec((Q_TILE, LANES), lambda hp, i: (i, hp)),
        out_shape=jax.ShapeDtypeStruct((s, d), BF16),
        scratch_shapes=[pltpu.VMEM((2, N_NEAR + 1, BLOCK, BLOCK), F32), pltpu.VMEM((2, 2, Q_TILE, Q_TILE), F32),
                        pltpu.VMEM((2, 2, Q_TILE, Q_TILE), BF16)],
        compiler_params=_params("arbitrary", "arbitrary"),
        name="moba_prompt")(thr, tab, q, kta, vb, kmt)


def _s5_body(x_ref, win_ref, bbr_ref, bbi_ref, cr_ref, ci_ref, ar_ref, ai_ref, d_ref, wglu_ref, bglu_ref,
             wout_ref, g_ref, b_ref, h0r_ref, h0i_ref, perm_ref, permt_ref, o_ref, hr_ref, hi_ref,
             u_ref, y_ref, sr_ref, si_ref, *chain_scratch, alpha, nseg, lseg, cw, chained):
    step = pl.program_id(0)
    slab = S5_SLAB * GROUP_P
    sw = S5_SLAB * STATE_N
    n_slab = win_ref.shape[1] // slab
    rows = lambda j: pl.ds(pl.multiple_of(j * nseg, 8), nseg)

    if chained:
        pwr_ref, pwi_ref, cyr_ref, cyi_ref = chain_scratch

        @pl.when(step == 0)
        def _():
            cyr_ref[...] = h0r_ref[...]
            cyi_ref[...] = h0i_ref[...]
            for c in range(ar_ref.shape[1] // sw):
                cs = slice(c * sw, (c + 1) * sw)
                ar, ai = ar_ref[:, cs], ai_ref[:, cs]

                def power(j, carry):
                    pr, pi = carry
                    pwr_ref[pl.ds(j, 1), cs] = pr
                    pwi_ref[pl.ds(j, 1), cs] = pi
                    return pr * ar - pi * ai, pr * ai + pi * ar
                lax.fori_loop(0, lseg, power, (ar, ai))

    x = x_ref[...]
    xb = _dot(perm_ref[...], x.astype(BF16)).astype(BF16)
    u_ref[...] = _dot(xb, win_ref[...])
    for s in range(n_slab):
        ub = u_ref[:, s * slab:(s + 1) * slab].astype(BF16)
        sr_ref[...] = _dot(ub, bbr_ref[s])
        si_ref[...] = _dot(ub, bbi_ref[s])
        for c in range(sw // cw):
            cs = slice(c * cw, (c + 1) * cw)
            gs = slice(s * sw + c * cw, s * sw + (c + 1) * cw)
            ar = jnp.broadcast_to(ar_ref[:, gs], (nseg, cw))
            ai = jnp.broadcast_to(ai_ref[:, gs], (nseg, cw))

            def scan(j, carry):
                hr, hi = carry
                nr = ar * hr - ai * hi + sr_ref[rows(j), cs]
                ni = ar * hi + ai * hr + si_ref[rows(j), cs]
                sr_ref[rows(j), cs] = nr
                si_ref[rows(j), cs] = ni
                return nr, ni

            if not chained:
                er, ei = lax.fori_loop(0, lseg, scan, (h0r_ref[:, gs], h0i_ref[:, gs]))
                hr_ref[:, gs] = er
                hi_ref[:, gs] = ei
            else:
                zero = jnp.zeros((nseg, cw), F32)
                er, ei = lax.fori_loop(0, lseg, scan, (zero, zero))
                plr, pli = pwr_ref[lseg - 1:lseg, gs], pwi_ref[lseg - 1:lseg, gs]
                cr, ci = cyr_ref[:, gs], cyi_ref[:, gs]
                hsr, hsi = [], []
                for r in range(nseg):
                    hsr.append(cr)
                    hsi.append(ci)
                    cr, ci = (er[r:r + 1] + plr * cr - pli * ci, ei[r:r + 1] + plr * ci + pli * cr)
                cyr_ref[:, gs] = cr
                cyi_ref[:, gs] = ci
                hsr = jnp.concatenate(hsr, axis=0)
                hsi = jnp.concatenate(hsi, axis=0)

                def fix(j, carry):
                    pr = jnp.broadcast_to(pwr_ref[pl.ds(j, 1), gs], (nseg, cw))
                    pi = jnp.broadcast_to(pwi_ref[pl.ds(j, 1), gs], (nseg, cw))
                    sr_ref[rows(j), cs] = sr_ref[rows(j), cs] + (pr * hsr - pi * hsi)
                    si_ref[rows(j), cs] = si_ref[rows(j), cs] + (pr * hsi + pi * hsr)
                    return carry
                lax.fori_loop(0, lseg, fix, 0)
        ys = _dot(sr_ref[...].astype(BF16), cr_ref[s]) - _dot(si_ref[...].astype(BF16), ci_ref[s])
        us = slice(s * slab, (s + 1) * slab)
        y_ref[:, us] = ys + d_ref[:, us] * u_ref[:, us]
    if chained:
        hr_ref[...] = cyr_ref[...]
        hi_ref[...] = cyi_ref[...]
    y = y_ref[...]
    z = jax.nn.gelu(y) * jax.nn.sigmoid(_dot(y.astype(BF16), wglu_ref[...]) + bglu_ref[...])
    zb = _dot(permt_ref[...], z.astype(BF16)).astype(BF16)
    m = _dot(zb, wout_ref[...])
    o_ref[...] = _layer_norm(alpha * x + m, g_ref[...], b_ref[...])


def s5_layer(x, h0r, h0i, w, g, b, alpha, *, nseg, lseg, cw, chained):
    m, d = x.shape
    tb = nseg * lseg
    n_state = h0r.shape[1]
    sw = S5_SLAB * STATE_N
    row = lambda i: (i, 0)
    hs = h0r.shape
    nat = jnp.arange(tb, dtype=I32)
    perm = ((nat[:, None] % nseg) * lseg + nat[:, None] // nseg == nat[None, :]).astype(BF16)
    scratch = [pltpu.VMEM((tb, d), F32), pltpu.VMEM((tb, d), F32), pltpu.VMEM((tb, sw), F32), pltpu.VMEM((tb, sw), F32)]
    if chained:
        scratch += [pltpu.VMEM((lseg, n_state), F32), pltpu.VMEM((lseg, n_state), F32),
                    pltpu.VMEM((1, n_state), F32), pltpu.VMEM((1, n_state), F32)]
    else:
        assert m == tb
    weights = [w["w_in"], w["bb_re"], w["bb_im"], w["c_re"], w["c_im"], w["a_re"], w["a_im"], w["d"],
               w["w_glu"], w["b_glu"], w["w_out"], g, b, h0r, h0i, perm, perm.T]
    return pl.pallas_call(
        functools.partial(_s5_body, alpha=alpha, nseg=nseg, lseg=lseg, cw=cw, chained=chained),
        grid=(m // tb,),
        in_specs=[pl.BlockSpec((tb, d), row)] + [_resident(a.shape) for a in weights],
        out_specs=[pl.BlockSpec((tb, d), row), pl.BlockSpec(hs, lambda i: (0, 0)), pl.BlockSpec(hs, lambda i: (0, 0))],
        out_shape=[jax.ShapeDtypeStruct((m, d), F32), jax.ShapeDtypeStruct(hs, F32), jax.ShapeDtypeStruct(hs, F32)],
        scratch_shapes=scratch, compiler_params=_params("arbitrary"), name="s5_layer")(x, *weights)


def s5_weights(w_in, lam_re, lam_im, log_dt, b_re, b_im, c_re, c_im, d, w_glu, b_glu, w_out):
    dt = jnp.exp(log_dt)[:, None]
    mag = jnp.exp(lam_re * dt)
    ang = lam_im * dt
    ab_re, ab_im = mag * jnp.cos(ang), mag * jnp.sin(ang)
    den = lam_re * lam_re + lam_im * lam_im
    f_re = ((ab_re - 1.0) * lam_re + ab_im * lam_im) / den
    f_im = (ab_im * lam_re - (ab_re - 1.0) * lam_im) / den
    bb_re = f_re[..., None] * b_re - f_im[..., None] * b_im
    bb_im = f_re[..., None] * b_im + f_im[..., None] * b_re
    n_g = lam_re.shape[0]
    eye = jnp.eye(S5_SLAB, dtype=F32)

    def in_slabs(bb):
        t = jnp.transpose(bb, (0, 2, 1)).reshape(n_g // S5_SLAB, S5_SLAB, GROUP_P, STATE_N)
        return jnp.einsum("sgpn,gh->sgphn", t, eye).reshape(n_g // S5_SLAB, S5_SLAB * GROUP_P, S5_SLAB * STATE_N).astype(BF16)

    def out_slabs(c):
        t = jnp.transpose(c, (0, 2, 1)).reshape(n_g // S5_SLAB, S5_SLAB, STATE_N, GROUP_P)
        return jnp.einsum("sgnp,gh->sgnhp", t, eye).reshape(n_g // S5_SLAB, S5_SLAB * STATE_N, S5_SLAB * GROUP_P).astype(BF16)

    return dict(w_in=w_in.astype(BF16), bb_re=in_slabs(bb_re), bb_im=in_slabs(bb_im), c_re=out_slabs(c_re),
                c_im=out_slabs(c_im), a_re=ab_re.reshape(1, -1), a_im=ab_im.reshape(1, -1), d=d.reshape(1, -1),
                w_glu=w_glu.astype(BF16), b_glu=b_glu.reshape(1, -1), w_out=w_out.astype(BF16))


def _sgate_body(pt_ref, q_ref, *rest, n_pg, nb):
    pages, idx_ref, km_ref = rest[:n_pg], rest[n_pg], rest[n_pg + 1]
    pc = pl.program_id(1)
    ppb = BLOCK // PAGE_SIZE
    lane = lax.broadcasted_iota(I32, km_ref.shape, 2)

    @pl.when(pc == 0)
    def _():
        km_ref[...] = jnp.zeros(km_ref.shape, F32)

    km = km_ref[...]
    for u in range(n_pg // ppb):
        tot = functools.reduce(jnp.add, [pages[u * ppb + t][0, 0] for t in range(ppb)])
        mean = jnp.sum(tot, axis=-1, keepdims=True) * (1.0 / BLOCK)
        km = jnp.where(lane == pc * (n_pg // ppb) + u, mean, km)
    km_ref[...] = km

    @pl.when(pc == pl.num_programs(1) - 1)
    def _():
        km = km_ref[...]
        bl = lax.broadcasted_iota(I32, (km.shape[0], km.shape[2]), 1)
        for qi in range(q_ref.shape[3]):
            gate = jnp.sum(km * q_ref[0, :, :, qi:qi + 1], axis=1)
            for t, am in enumerate(_top3(jnp.where(bl < nb, gate, -jnp.inf), bl, 1)):
                idx_ref[0, qi, :, t:t + 1] = am


def sample_gate(qt, cache_kt, page_table, li):
    db, nh, dh, nq = qt.shape
    n_pages = page_table.shape[1]
    nb = n_pages * PAGE_SIZE // BLOCK
    assert nb <= LANES
    n_pg = 16
    page = lambda t: pl.BlockSpec((1, 1, nh, dh, PAGE_SIZE),
                                  lambda b, pc, pt: (li, pt[b * n_pages + pc * n_pg + t], 0, 0, 0))
    return pl.pallas_call(
        functools.partial(_sgate_body, n_pg=n_pg, nb=nb),
        grid_spec=pltpu.PrefetchScalarGridSpec(
            num_scalar_prefetch=1, grid=(db, n_pages // n_pg),
            in_specs=[pl.BlockSpec((1, nh, dh, nq), lambda b, pc, pt: (b, 0, 0, 0))] + [page(t) for t in range(n_pg)],
            out_specs=pl.BlockSpec((1, nq, nh, TOPK), lambda b, pc, pt: (b, 0, 0, 0)),
            scratch_shapes=[pltpu.VMEM((nh, dh, LANES), F32)]),
        out_shape=jax.ShapeDtypeStruct((db, nq, nh, TOPK), I32),
        compiler_params=_params("arbitrary", "arbitrary"), name="sample_gate")(
            page_table.reshape(-1), qt, *([cache_kt] * n_pg))


def _sattn_body(pt_ref, idx_ref, q_ref, kn_ref, vn_ref, lo_ref, hi_ref, tab2_ref, ck_hbm, cv_hbm,
                o_ref, kbuf, vbuf, sem, *, nq, nh, n_pages, li, past_len):
    ppb = BLOCK // PAGE_SIZE
    b = pl.program_id(0)
    off = lax.broadcasted_iota(I32, (1, BLOCK), 1)
    head_lane = lax.broadcasted_iota(I32, tab2_ref.shape, 1)
    lo, hi = lo_ref[...], hi_ref[...]
    d_new = lax.broadcasted_iota(I32, (nq, nq), 0) - lax.broadcasted_iota(I32, (nq, nq), 1)

    def page_copies(bb, h, slot, fetch):
        out = []
        for qi in range(nq):
            for t in range(TOPK):
                blk = idx_ref[((bb * nq + qi) * nh + h) * TOPK + t] if fetch else 0
                for pp in range(ppb):
                    page = pt_ref[bb * n_pages + blk * ppb + pp] if fetch else 0
                    j = (qi * TOPK + t) * ppb + pp
                    out.append(pltpu.make_async_copy(ck_hbm.at[li, page, h], kbuf.at[slot, j], sem.at[slot]))
                    out.append(pltpu.make_async_copy(cv_hbm.at[li, page, h], vbuf.at[slot, j], sem.at[slot]))
        return out

    @pl.when(b == 0)
    def _():
        for c in page_copies(0, 0, 0, True):
            c.start()

    def head(h, carry):
        slot = h % 2
        last_head = h == nh - 1
        nb_, nh_ = jnp.where(last_head, b + 1, b), jnp.where(last_head, 0, h + 1)

        @pl.when(nb_ < pl.num_programs(0))
        def _():
            for c in page_copies(nb_, nh_, 1 - slot, True):
                c.start()

        for c in page_copies(0, 0, slot, False):
            c.wait()

        tabcol = jnp.sum(jnp.where(head_lane == h, tab2_ref[...], 0.0), axis=1, keepdims=True)

        def bias_row(dist):
            return jnp.sum(jnp.where((dist >= lo) & (dist < hi), tabcol, 0.0), axis=0, keepdims=True)

        kn, vn = kn_ref[0, h], vn_ref[0, h]
        for qi in range(nq):
            qcol = q_ref[0, h, :, qi:qi + 1] * QK_SCALE
            scores = []
            for t in range(TOPK):
                blk = idx_ref[((b * nq + qi) * nh + h) * TOPK + t]
                bias = bias_row((past_len + qi) - (blk * BLOCK + off))
                for pp in range(ppb):
                    kt = kbuf[slot, (qi * TOPK + t) * ppb + pp]
                    scores.append(jnp.sum(kt * qcol, axis=0, keepdims=True)
                                  + bias[:, pp * PAGE_SIZE:(pp + 1) * PAGE_SIZE])
            s_own = jnp.sum(kn * qcol, axis=0, keepdims=True) + bias_row(d_new[qi:qi + 1, :])
            scores.append(jnp.where(d_new[qi:qi + 1, :] >= 0, s_own, NEG_INF))
            m = functools.reduce(jnp.maximum, [jnp.max(s, axis=1, keepdims=True) for s in scores])
            ps = [jnp.exp(s - m) for s in scores]
            l = functools.reduce(jnp.add, [jnp.sum(p, axis=1, keepdims=True) for p in ps])
            acc = functools.reduce(jnp.add, [vbuf[slot, qi * TOPK * ppb + j] * (ps[j] / l) for j in range(TOPK * ppb)])
            o = jnp.sum(acc, axis=1, keepdims=True) + jnp.sum(vn * (ps[-1] / l), axis=1, keepdims=True)
            o_ref[0, h, :, qi:qi + 1] = o
        return carry

    lax.fori_loop(0, nh, head, 0)


def sample_attn(qt, knt, vnt, cache_kt, cache_vt, page_table, idx, thr, tab, li, past_len):
    db, nh, dh, nq = qt.shape
    assert nh % 2 == 0
    n_pages = page_table.shape[1]
    n_pg = nq * TOPK * (BLOCK // PAGE_SIZE)
    tok = pl.BlockSpec((1, nh, dh, nq), lambda b, pt, ix: (b, 0, 0, 0))
    hbm = pl.BlockSpec(memory_space=pl.ANY)
    whole = lambda a: pl.BlockSpec(a.shape, lambda b, pt, ix: (0,) * a.ndim)
    lo = thr.reshape(-1, 1)
    hi = jnp.concatenate([thr[1:], jnp.full((1,), jnp.iinfo(jnp.int32).max, I32)]).reshape(-1, 1)
    return pl.pallas_call(
        functools.partial(_sattn_body, nq=nq, nh=nh, n_pages=n_pages, li=li, past_len=past_len),
        grid_spec=pltpu.PrefetchScalarGridSpec(
            num_scalar_prefetch=2, grid=(db,),
            in_specs=[tok, tok, tok, whole(lo), whole(hi), whole(tab), hbm, hbm],
            out_specs=tok,
            scratch_shapes=[pltpu.VMEM((2, n_pg, dh, PAGE_SIZE), F32), pltpu.VMEM((2, n_pg, dh, PAGE_SIZE), F32),
                            pltpu.SemaphoreType.DMA((2,))]),
        out_shape=jax.ShapeDtypeStruct((db, nh, dh, nq), F32),
        compiler_params=_params("arbitrary"), name="sample_attn")(
            page_table.reshape(-1), idx.reshape(-1), qt, knt, vnt, lo, hi, tab, cache_kt, cache_vt)


def _bucket_thresholds():
    n = jnp.arange(MAX_DISTANCE + 1, dtype=I32)
    nf = jnp.maximum(n, 1).astype(F32)
    log_b = MAX_EXACT + (jnp.log(nf / MAX_EXACT) / math.log(MAX_DISTANCE / MAX_EXACT)
                         * (N_BUCKETS - MAX_EXACT)).astype(I32)
    bucket = jnp.where(n < MAX_EXACT, n, jnp.minimum(log_b, N_BUCKETS - 1))
    return jnp.sum(bucket[None, :] < jnp.arange(N_BUCKETS, dtype=I32)[:, None], axis=1).astype(I32)


def _heads(t, b, s):
    return t.reshape(b, s, N_HEADS, HEAD_DIM).transpose(0, 2, 1, 3)


def kernel(x_prompt, x_sample, cache_k, cache_v, state_s5_re, state_s5_im, page_table, p_prompt, p_sample, ln_g, ln_b, ffn_w1, ffn_w3, ffn_w2, ple_w_proj, ple_w_gate, rel_bias_table, attn_w_qkv, attn_w_o, s5_w_in, s5_lambda_re, s5_lambda_im, s5_log_dt, s5_b_re, s5_b_im, s5_c_re, s5_c_im, s5_d, s5_w_glu, s5_b_glu, s5_w_out):
    depth = ln_g.shape[0]
    alpha = (2 * depth) ** 0.25
    bsz, seq, d = x_prompt.shape
    db, nq, _ = x_sample.shape
    assert bsz == 1 and seq % (2 * BLOCK) == 0 and d == N_HEADS * HEAD_DIM
    past_len = page_table.shape[1] * PAGE_SIZE
    assert past_len % BLOCK == 0
    xp = x_prompt.reshape(seq, d)
    xs = x_sample.reshape(db * nq, d)
    thr = _bucket_thresholds()
    tab = rel_bias_table.reshape(-1)
    vec = lambda a: a.reshape(1, -1)
    outs = {k: [] for k in ("kp", "vp", "ks", "vs", "spr", "spi", "ssr", "ssi")}
    for i in range(depth):
        li = i // 2
        w1, w3, w2 = ffn_w1[i].astype(BF16), ffn_w3[i].astype(BF16), ffn_w2[i].astype(BF16)
        g, b = ln_g[i], ln_b[i]
        xp = ffn_ln(xp, w1[0], w3[0], w2[0], vec(g[0]), vec(b[0]), alpha)
        xs = ffn_ln(xs, w1[0], w3[0], w2[0], vec(g[0]), vec(b[0]), alpha)
        if i % 2 == 0:
            wqkv = attn_w_qkv[li].astype(BF16)
            wq, wk, wv = wqkv[:, :d], wqkv[:, d:2 * d], wqkv[:, 2 * d:]
            wo = attn_w_o[li].astype(BF16)
            q, kto, vto, kta, vb, km = qkv_prompt(xp, wq, wv, wk.T, wv.T)
            kmt = jnp.transpose(km, (1, 0, 2)).reshape(d, seq // BLOCK)
            o = moba_prompt(q, kta, vb, kmt, thr, tab)
            xp = lin_ln(xp, o, wo, vec(g[1]), vec(b[1]), alpha)
            outs["kp"].append(jnp.swapaxes(kto.reshape(1, N_HEADS, HEAD_DIM, seq), 2, 3))
            outs["vp"].append(jnp.swapaxes(vto.reshape(1, N_HEADS, HEAD_DIM, seq), 2, 3))

            qkv_s = linear(xs, wqkv)
            qs, ks, vs = (_heads(t, db, nq) for t in jnp.split(qkv_s, 3, axis=-1))
            ckt, cvt = jnp.swapaxes(cache_k, 3, 4), jnp.swapaxes(cache_v, 3, 4)
            qt, knt, vnt = (jnp.swapaxes(t, 2, 3) for t in (qs, ks, vs))
            idx = sample_gate(qt, ckt, page_table, li)
            o_s = sample_attn(qt, knt, vnt, ckt, cvt, page_table, idx, thr, rel_bias_table, li, past_len)
            xs = lin_ln(xs, o_s.transpose(0, 3, 1, 2).reshape(db * nq, d), wo, vec(g[1]), vec(b[1]), alpha)
            outs["ks"].append(ks)
            outs["vs"].append(vs)
        else:
            w = s5_weights(s5_w_in[li], s5_lambda_re[li], s5_lambda_im[li], s5_log_dt[li], s5_b_re[li], s5_b_im[li],
                           s5_c_re[li], s5_c_im[li], s5_d[li], s5_w_glu[li], s5_b_glu[li], s5_w_out[li])
            n_state = w["a_re"].shape[1]
            zero = jnp.zeros((1, n_state), F32)
            xp, hr, hi = s5_layer(xp, zero, zero, w, vec(g[1]), vec(b[1]), alpha, nseg=8, lseg=64, cw=1024, chained=True)
            outs["spr"].append(hr.reshape(1, -1, STATE_N))
            outs["spi"].append(hi.reshape(1, -1, STATE_N))
            xs, hr, hi = s5_layer(xs, state_s5_re[li].reshape(db, n_state), state_s5_im[li].reshape(db, n_state), w,
                                  vec(g[1]), vec(b[1]), alpha, nseg=db, lseg=nq, cw=256, chained=False)
            outs["ssr"].append(hr.reshape(db, -1, STATE_N))
            outs["ssi"].append(hi.reshape(db, -1, STATE_N))
        ple_w = (ple_w_gate[i].astype(BF16), ple_w_proj[i].astype(BF16))
        xp = ffn_ln(xp, w1[1], w3[1], w2[1], vec(g[2]), vec(b[2]), alpha, ple=(p_prompt[i].reshape(seq, -1),) + ple_w)
        xs = ffn_ln(xs, w1[1], w3[1], w2[1], vec(g[2]), vec(b[2]), alpha, ple=(p_sample[i].reshape(db * nq, -1),) + ple_w)
    st = lambda k: jnp.stack(outs[k])
    return (xp.reshape(bsz, seq, d), xs.reshape(db, nq, d), st("kp"), st("vp"), st("ks"), st("vs"),
            st("spr"), st("spi"), st("ssr"), st("ssi"))
```

```python
import functools
import math

import jax
import jax.numpy as jnp
from jax import lax
from jax.experimental import pallas as pl
from jax.experimental.pallas import tpu as pltpu

F32 = jnp.float32
BF16 = jnp.bfloat16
I32 = jnp.int32

N_HEADS = 16
HEAD_DIM = 64
BLOCK = 256
TOPK = 3
PAGE_SIZE = 128
N_BUCKETS = 32
MAX_EXACT = N_BUCKETS // 2
MAX_DISTANCE = 4096
GROUP_P = 16
STATE_N = 64
LN_EPS = 1e-5
NEG_INF = -1e30
MASK_BIG = 1e30
QK_SCALE = HEAD_DIM ** -0.5
LOG2E = math.log2(math.e)

LANES = 128
VMEM_LIMIT = 56 * 1024 * 1024

N_NEAR = MAX_DISTANCE // BLOCK + 1
Q_TILE = 2 * BLOCK
N_FAR_PAIR = (N_NEAR + 2) // 2
S5_SLAB = 16


def _params(*sem, flags=None):
    return pltpu.CompilerParams(dimension_semantics=sem, vmem_limit_bytes=VMEM_LIMIT, flags=flags)


def _resident(shape):
    nd = len(shape)
    return pl.BlockSpec(shape, lambda *_: (0,) * nd, pipeline_mode=pl.Buffered(1))


def _dot(a, b):
    return jnp.dot(a, b, preferred_element_type=F32)


def _layer_norm(y, g, b):
    mu = jnp.mean(y, axis=-1, keepdims=True)
    yc = y - mu
    var = jnp.mean(yc * yc, axis=-1, keepdims=True)
    return yc * lax.rsqrt(var + LN_EPS) * g + b


def _top3(work, cols, axis):
    out = []
    cols = cols.astype(F32)
    for _ in range(TOPK):
        mx = jnp.max(work, axis=axis, keepdims=True)
        am = jnp.min(jnp.where(work == mx, cols, jnp.inf), axis=axis, keepdims=True)
        out.append(jnp.where(mx > -jnp.inf, am, -1.0).astype(I32))
        work = jnp.where(cols == am, -jnp.inf, work)
    return out


def _bias_of(dist, head, thr_ref, tab_ref, unroll=False):
    def step(b, acc):
        return jnp.where(dist >= thr_ref[b], tab_ref[b * N_HEADS + head], acc)
    acc = jnp.full(dist.shape, tab_ref[head], F32)
    if unroll:
        for b in range(1, N_BUCKETS):
            acc = step(b, acc)
        return acc
    return lax.fori_loop(1, N_BUCKETS, step, acc)


def _ffn_body(x_ref, w1_ref, w3_ref, w2_ref, g_ref, b_ref, *rest, alpha, tf, ple):
    if ple:
        p_ref, wg_ref, wp_ref, o_ref = rest
    else:
        (o_ref,) = rest
    x = x_ref[...]
    xb = x.astype(BF16)
    acc = jnp.zeros(x.shape, F32)
    for j in range(w1_ref.shape[1] // tf):
        sl = slice(j * tf, (j + 1) * tf)
        h = jax.nn.silu(_dot(xb, w1_ref[:, sl])) * _dot(xb, w3_ref[:, sl])
        acc = acc + _dot(h.astype(BF16), w2_ref[sl, :])
    y = _layer_norm(alpha * x + 0.5 * acc, g_ref[...], b_ref[...])
    if ple:
        gate = jax.nn.sigmoid(_dot(y.astype(BF16), wg_ref[...]))
        y = y + gate * _dot(p_ref[...].astype(BF16), wp_ref[...])
    o_ref[...] = y


def ffn_ln(x, w1, w3, w2, g, b, alpha, ple=None):
    m, d = x.shape
    tm = min(512, m)
    row = lambda i: (i, 0)
    in_specs = [pl.BlockSpec((tm, d), row), _resident(w1.shape), _resident(w3.shape), _resident(w2.shape),
                _resident(g.shape), _resident(b.shape)]
    args = [x, w1, w3, w2, g, b]
    if ple is not None:
        p, wg, wp = ple
        in_specs += [pl.BlockSpec((tm, p.shape[1]), row), _resident(wg.shape), _resident(wp.shape)]
        args += [p, wg, wp]
    return pl.pallas_call(
        functools.partial(_ffn_body, alpha=alpha, tf=2 * LANES, ple=ple is not None),
        grid=(m // tm,), in_specs=in_specs, out_specs=pl.BlockSpec((tm, d), row),
        out_shape=jax.ShapeDtypeStruct((m, d), F32), compiler_params=_params("parallel"),
        name="ffn_ln")(*args)


def _lin_ln_body(x_ref, a_ref, w_ref, g_ref, b_ref, o_ref, *, alpha):
    m = _dot(a_ref[...].astype(BF16), w_ref[...])
    o_ref[...] = _layer_norm(alpha * x_ref[...] + m, g_ref[...], b_ref[...])


def lin_ln(x, a, w, g, b, alpha):
    m, d = x.shape
    tm = min(512, m)
    row = lambda i: (i, 0)
    return pl.pallas_call(
        functools.partial(_lin_ln_body, alpha=alpha),
        grid=(m // tm,),
        in_specs=[pl.BlockSpec((tm, d), row), pl.BlockSpec((tm, a.shape[1]), row), _resident(w.shape),
                  _resident(g.shape), _resident(b.shape)],
        out_specs=pl.BlockSpec((tm, d), row), out_shape=jax.ShapeDtypeStruct((m, d), F32),
        compiler_params=_params("parallel"), name="lin_ln")(x, a, w, g, b)


def _linear_body(x_ref, w_ref, o_ref):
    o_ref[...] = _dot(x_ref[...].astype(BF16), w_ref[...])


def linear(x, w):
    m = x.shape[0]
    return pl.pallas_call(
        _linear_body, grid=(1,), in_specs=[_resident(x.shape), _resident(w.shape)],
        out_specs=pl.BlockSpec((m, w.shape[1]), lambda i: (0, 0)),
        out_shape=jax.ShapeDtypeStruct((m, w.shape[1]), F32), compiler_params=_params("arbitrary"),
        name="linear")(x, w)


def _qkv_body(x_ref, wq_ref, wv_ref, wkt_ref, wvt_ref, q_ref, kto_ref, vto_ref, kta_ref, vb_ref, km_ref):
    xb = x_ref[...].astype(BF16)
    tm, d = xb.shape
    nblk = tm // BLOCK
    q_ref[...] = _dot(xb, wq_ref[...])
    vb_ref[...] = _dot(xb, wv_ref[...]).astype(BF16)
    nt = lambda w_ref: lax.dot_general(w_ref[...], xb, (((1,), (1,)), ((), ())), preferred_element_type=F32)
    kt = nt(wkt_ref)
    kto_ref[...] = kt
    vto_ref[...] = nt(wvt_ref)
    ktb = kt.astype(BF16)
    row = lax.broadcasted_iota(I32, (LANES, BLOCK), 0)
    for bk in range(nblk):
        n = pl.program_id(0) * nblk + bk
        cols = slice(bk * BLOCK, (bk + 1) * BLOCK)
        km_ref[0, :, bk:bk + 1] = jnp.sum(kt[:, cols], axis=1, keepdims=True) * (1.0 / BLOCK)
        mask_rows = jnp.where(row % HEAD_DIM == n, MASK_BIG, 0.0).astype(BF16)
        for p in range(d // LANES):
            pair = ktb[p * LANES:(p + 1) * LANES, cols]
            kta_ref[2 * p, bk] = jnp.where(row < HEAD_DIM, pair, mask_rows)
            kta_ref[2 * p + 1, bk] = jnp.where(row >= HEAD_DIM, pair, mask_rows)


def qkv_prompt(x, wq, wv, wkt, wvt):
    s, d = x.shape
    tm = 2 * BLOCK
    nb = s // BLOCK
    assert nb <= HEAD_DIM
    row = lambda i: (i, 0)
    col = lambda i: (0, i)
    return pl.pallas_call(
        _qkv_body, grid=(s // tm,),
        in_specs=[pl.BlockSpec((tm, d), row)] + [_resident(w.shape) for w in (wq, wv, wkt, wvt)],
        out_specs=[pl.BlockSpec((tm, d), row),
                   pl.BlockSpec((d, tm), col),
                   pl.BlockSpec((d, tm), col),
                   pl.BlockSpec((N_HEADS, tm // BLOCK, LANES, BLOCK), lambda i: (0, i, 0, 0)),
                   pl.BlockSpec((tm, d), row),
                   pl.BlockSpec((1, d, tm // BLOCK), lambda i: (i, 0, 0))],
        out_shape=[jax.ShapeDtypeStruct((s, d), F32),
                   jax.ShapeDtypeStruct((d, s), F32),
                   jax.ShapeDtypeStruct((d, s), F32),
                   jax.ShapeDtypeStruct((N_HEADS, nb, LANES, BLOCK), BF16),
                   jax.ShapeDtypeStruct((s, d), BF16),
                   jax.ShapeDtypeStruct((s // tm, d, tm // BLOCK), F32)],
        compiler_params=_params("parallel"), name="qkv_prompt")(x, wq, wv, wkt, wvt)


def _split_bf16(x):
    hi = x.astype(BF16)
    return hi, (x - hi.astype(F32)).astype(BF16)


def _moba_body(thr_ref, tab_ref, q_ref, kt_ref, v_ref, km_ref, o_ref, bias_ref, s_ref, p_ref):
    hp = pl.program_id(0)
    it = pl.program_id(1)
    nb = km_ref.shape[1]
    rr = lax.broadcasted_iota(I32, (BLOCK, BLOCK), 0)
    cc = lax.broadcasted_iota(I32, (BLOCK, BLOCK), 1)

    @pl.when(it == 0)
    def _():
        for hh in range(2):
            head = hp * 2 + hh

            def fill(dl, carry):
                bias_ref[hh, dl] = _bias_of(rr - cc + dl * BLOCK, head, thr_ref, tab_ref) * LOG2E
                return carry
            lax.fori_loop(0, N_NEAR, fill, 0)
            bias_ref[hh, N_NEAR] = jnp.full((BLOCK, BLOCK), tab_ref[(N_BUCKETS - 1) * N_HEADS + head] * LOG2E, F32)

    lane = lax.broadcasted_iota(I32, (Q_TILE, LANES), 1)
    own = 2 * it + lax.broadcasted_iota(I32, (Q_TILE, 1), 0) // BLOCK
    bcol = lax.broadcasted_iota(I32, (Q_TILE, nb), 1)
    causal = rr >= cc
    q = q_ref[...]
    km_hi, km_lo = _split_bf16(km_ref[...])

    def keys(hh, blk):
        return kt_ref[hh, blk]

    vlane = lax.broadcasted_iota(I32, (Q_TILE, LANES), 1)

    def attend(hh, p, pair):
        v = v_ref[pl.ds(pl.multiple_of(pair * Q_TILE, Q_TILE), Q_TILE), :]
        mine = (vlane >= hh * HEAD_DIM) & (vlane < (hh + 1) * HEAD_DIM)
        return _dot(p, jnp.where(mine, v, jnp.ones_like(v)))

    lhs, state, c_far = [], [], []
    for hh in range(2):
        in_q = (lane >= hh * HEAD_DIM) & (lane < (hh + 1) * HEAD_DIM)
        qh = jnp.where(in_q, q, 0.0)
        q_hi, q_lo = _split_bf16(qh)
        gate = _dot(q_hi, km_hi) + _dot(q_hi, km_lo) + _dot(q_lo, km_hi)
        i1, i2, i3 = _top3(jnp.where(bcol < own, gate, -jnp.inf), bcol, 1)
        e = lane - (1 - hh) * HEAD_DIM
        unsel = jnp.where((e == i1) | (e == i2) | (e == i3), 0.0, -1.0)
        qs = qh * (QK_SCALE * LOG2E)
        lhs.append(jnp.where(in_q, qs, unsel).astype(BF16))
        c_far.append(tab_ref[(N_BUCKETS - 1) * N_HEADS + hp * 2 + hh] * LOG2E)

        lhs_own = jnp.where(in_q, qs, jnp.where(e == own, 0.0, unsel)).astype(BF16)
        ss = []
        for b in range(2):
            s = _dot(lhs_own, keys(hh, 2 * it + b))
            top = s[:BLOCK] + bias_ref[hh, 0]
            bot = s[BLOCK:] + bias_ref[hh, 1 - b]
            if b == 0:
                top = jnp.where(causal, top, NEG_INF)
            else:
                bot = jnp.where(causal, bot, NEG_INF)
            ss.append(jnp.concatenate([top, bot], axis=0))
        m = jnp.max(jnp.maximum(ss[0], ss[1]), axis=-1, keepdims=True)
        p = jnp.concatenate([jnp.exp2(s - m).astype(BF16) for s in ss], axis=1)
        state += [m, attend(hh, p, it)]

    def step(jp, carry, near, slot=0):
        for hh in range(2):
            for b in range(2):
                s_ref[slot, hh, :, b * BLOCK:(b + 1) * BLOCK] = _dot(lhs[hh], keys(hh, 2 * jp + b))
        out = []
        for hh in range(2):
            m, acc = carry[2 * hh:2 * hh + 2]

            def scores(halves):
                s = (jnp.concatenate([s_ref[slot, hh, :, :BLOCK], s_ref[slot, hh, :, BLOCK:]], axis=1) if halves
                     else s_ref[slot, hh])
                if near:
                    dl = 2 * (it - jp)
                    tile = lambda a, b: bias_ref[hh, jnp.minimum(dl + a - b, N_NEAR)]
                    s = s + jnp.concatenate([jnp.concatenate([tile(0, 0), tile(0, 1)], axis=1),
                                             jnp.concatenate([tile(1, 0), tile(1, 1)], axis=1)], axis=0)
                return s

            rmax = jnp.max(scores(False), axis=-1, keepdims=True)
            if near:
                m_new = jnp.maximum(m, rmax)
                shift = m_new
            else:
                m_new = jnp.maximum(m, rmax + c_far[hh])
                shift = m_new - c_far[hh]
            p_ref[slot, hh] = jnp.exp2(scores(True) - shift).astype(BF16)
            out += [m_new, jnp.exp2(m - m_new) * acc + attend(hh, p_ref[slot, hh], jp)]
        return tuple(out)

    def run(lo, hi, carry, near):
        def four(t, c):
            for u in range(4):
                c = step(lo + 4 * t + u, c, near, u % 2)
            return c
        n4 = (hi - lo) // 4
        carry = lax.fori_loop(0, n4, four, carry)
        return lax.fori_loop(lo + 4 * n4, hi, functools.partial(step, near=near), carry)

    n_far = jnp.maximum(it - (N_FAR_PAIR - 1), 0)
    state = run(0, n_far, tuple(state), False)
    state = run(n_far, it, state, True)
    l0 = state[1][:, HEAD_DIM:HEAD_DIM + 1]
    l1 = state[3][:, 0:1]
    o_ref[...] = jnp.where(lane < HEAD_DIM, state[1] / l0, state[3] / l1).astype(BF16)


def moba_prompt(q, kta, vb, kmt, thr, tab):
    s, d = q.shape
    nb = s // BLOCK
    smem = pl.BlockSpec(memory_space=pltpu.SMEM)
    return pl.pallas_call(
        _moba_body, grid=(d // LANES, s // Q_TILE),
        in_specs=[smem, smem,
                  pl.BlockSpec((Q_TILE, LANES), lambda hp, i: (i, hp)),
                  pl.BlockSpec((2, nb, LANES, BLOCK), lambda hp, i: (hp, 0, 0, 0)),
                  pl.BlockSpec((s, LANES), lambda hp, i: (0, hp)),
                  pl.BlockSpec((LANES, nb), lambda hp, i: (hp, 0))],
        out_specs=pl.BlockSpec((Q_TILE, LANES), lambda hp, i: (i, hp)),
        out_shape=jax.ShapeDtypeStruct((s, d), BF16),
        scratch_shapes=[pltpu.VMEM((2, N_NEAR + 1, BLOCK, BLOCK), F32), pltpu.VMEM((2, 2, Q_TILE, Q_TILE), F32),
                        pltpu.VMEM((2, 2, Q_TILE, Q_TILE), BF16)],
        compiler_params=_params("arbitrary", "arbitrary"),
        name="moba_prompt")(thr, tab, q, kta, vb, kmt)


def _s5_body(x_ref, win_ref, bbr_ref, bbi_ref, cr_ref, ci_ref, ar_ref, ai_ref, d_ref, wglu_ref, bglu_ref,
             wout_ref, g_ref, b_ref, h0r_ref, h0i_ref, perm_ref, permt_ref, o_ref, hr_ref, hi_ref,
             u_ref, y_ref, sr_ref, si_ref, *chain_scratch, alpha, nseg, lseg, cw, chained):
    step = pl.program_id(0)
    slab = S5_SLAB * GROUP_P
    sw = S5_SLAB * STATE_N
    n_slab = win_ref.shape[1] // slab
    rows = lambda j: pl.ds(pl.multiple_of(j * nseg, 8), nseg)

    if chained:
        pwr_ref, pwi_ref, cyr_ref, cyi_ref = chain_scratch

        @pl.when(step == 0)
        def _():
            cyr_ref[...] = h0r_ref[...]
            cyi_ref[...] = h0i_ref[...]
            for c in range(ar_ref.shape[1] // sw):
                cs = slice(c * sw, (c + 1) * sw)
                ar, ai = ar_ref[:, cs], ai_ref[:, cs]

                def power(j, carry):
                    pr, pi = carry
                    pwr_ref[pl.ds(j, 1), cs] = pr
                    pwi_ref[pl.ds(j, 1), cs] = pi
                    return pr * ar - pi * ai, pr * ai + pi * ar
                lax.fori_loop(0, lseg, power, (ar, ai))

    x = x_ref[...]
    xb = _dot(perm_ref[...], x.astype(BF16)).astype(BF16)
    u_ref[...] = _dot(xb, win_ref[...])
    for s in range(n_slab):
        ub = u_ref[:, s * slab:(s + 1) * slab].astype(BF16)
        sr_ref[...] = _dot(ub, bbr_ref[s])
        si_ref[...] = _dot(ub, bbi_ref[s])
        for c in range(sw // cw):
            cs = slice(c * cw, (c + 1) * cw)
            gs = slice(s * sw + c * cw, s * sw + (c + 1) * cw)
            ar = jnp.broadcast_to(ar_ref[:, gs], (nseg, cw))
            ai = jnp.broadcast_to(ai_ref[:, gs], (nseg, cw))

            def scan(j, carry):
                hr, hi = carry
                nr = ar * hr - ai * hi + sr_ref[rows(j), cs]
                ni = ar * hi + ai * hr + si_ref[rows(j), cs]
                sr_ref[rows(j), cs] = nr
                si_ref[rows(j), cs] = ni
                return nr, ni

            if not chained:
                er, ei = lax.fori_loop(0, lseg, scan, (h0r_ref[:, gs], h0i_ref[:, gs]))
                hr_ref[:, gs] = er
                hi_ref[:, gs] = ei
            else:
                zero = jnp.zeros((nseg, cw), F32)
                er, ei = lax.fori_loop(0, lseg, scan, (zero, zero))
                plr, pli = pwr_ref[lseg - 1:lseg, gs], pwi_ref[lseg - 1:lseg, gs]
                cr, ci = cyr_ref[:, gs], cyi_ref[:, gs]
                hsr, hsi = [], []
                for r in range(nseg):
                    hsr.append(cr)
                    hsi.append(ci)
                    cr, ci = (er[r:r + 1] + plr * cr - pli * ci, ei[r:r + 1] + plr * ci + pli * cr)
                cyr_ref[:, gs] = cr
                cyi_ref[:, gs] = ci
                hsr = jnp.concatenate(hsr, axis=0)
                hsi = jnp.concatenate(hsi, axis=0)

                def fix(j, carry):
                    pr = jnp.broadcast_to(pwr_ref[pl.ds(j, 1), gs], (nseg, cw))
                    pi = jnp.broadcast_to(pwi_ref[pl.ds(j, 1), gs], (nseg, cw))
                    sr_ref[rows(j), cs] = sr_ref[rows(j), cs] + (pr * hsr - pi * hsi)
                    si_ref[rows(j), cs] = si_ref[rows(j), cs] + (pr * hsi + pi * hsr)
                    return carry
                lax.fori_loop(0, lseg, fix, 0)
        ys = _dot(sr_ref[...].astype(BF16), cr_ref[s]) - _dot(si_ref[...].astype(BF16), ci_ref[s])
        us = slice(s * slab, (s + 1) * slab)
        y_ref[:, us] = ys + d_ref[:, us] * u_ref[:, us]
    if chained:
        hr_ref[...] = cyr_ref[...]
        hi_ref[...] = cyi_ref[...]
    y = y_ref[...]
    z = jax.nn.gelu(y) * jax.nn.sigmoid(_dot(y.astype(BF16), wglu_ref[...]) + bglu_ref[...])
    zb = _dot(permt_ref[...], z.astype(BF16)).astype(BF16)
    m = _dot(zb, wout_ref[...])
    o_ref[...] = _layer_norm(alpha * x + m, g_ref[...], b_ref[...])


def s5_layer(x, h0r, h0i, w, g, b, alpha, *, nseg, lseg, cw, chained):
    m, d = x.shape
    tb = nseg * lseg
    n_state = h0r.shape[1]
    sw = S5_SLAB * STATE_N
    row = lambda i: (i, 0)
    hs = h0r.shape
    nat = jnp.arange(tb, dtype=I32)
    perm = ((nat[:, None] % nseg) * lseg + nat[:, None] // nseg == nat[None, :]).astype(BF16)
    scratch = [pltpu.VMEM((tb, d), F32), pltpu.VMEM((tb, d), F32), pltpu.VMEM((tb, sw), F32), pltpu.VMEM((tb, sw), F32)]
    if chained:
        scratch += [pltpu.VMEM((lseg, n_state), F32), pltpu.VMEM((lseg, n_state), F32),
                    pltpu.VMEM((1, n_state), F32), pltpu.VMEM((1, n_state), F32)]
    else:
        assert m == tb
    weights = [w["w_in"], w["bb_re"], w["bb_im"], w["c_re"], w["c_im"], w["a_re"], w["a_im"], w["d"],
               w["w_glu"], w["b_glu"], w["w_out"], g, b, h0r, h0i, perm, perm.T]
    return pl.pallas_call(
        functools.partial(_s5_body, alpha=alpha, nseg=nseg, lseg=lseg, cw=cw, chained=chained),
        grid=(m // tb,),
        in_specs=[pl.BlockSpec((tb, d), row)] + [_resident(a.shape) for a in weights],
        out_specs=[pl.BlockSpec((tb, d), row), pl.BlockSpec(hs, lambda i: (0, 0)), pl.BlockSpec(hs, lambda i: (0, 0))],
        out_shape=[jax.ShapeDtypeStruct((m, d), F32), jax.ShapeDtypeStruct(hs, F32), jax.ShapeDtypeStruct(hs, F32)],
        scratch_shapes=scratch, compiler_params=_params("arbitrary"), name="s5_layer")(x, *weights)


def s5_weights(w_in, lam_re, lam_im, log_dt, b_re, b_im, c_re, c_im, d, w_glu, b_glu, w_out):
    dt = jnp.exp(log_dt)[:, None]
    mag = jnp.exp(lam_re * dt)
    ang = lam_im * dt
    ab_re, ab_im = mag * jnp.cos(ang), mag * jnp.sin(ang)
    den = lam_re * lam_re + lam_im * lam_im
    f_re = ((ab_re - 1.0) * lam_re + ab_im * lam_im) / den
    f_im = (ab_im * lam_re - (ab_re - 1.0) * lam_im) / den
    bb_re = f_re[..., None] * b_re - f_im[..., None] * b_im
    bb_im = f_re[..., None] * b_im + f_im[..., None] * b_re
    n_g = lam_re.shape[0]
    eye = jnp.eye(S5_SLAB, dtype=F32)

    def in_slabs(bb):
        t = jnp.transpose(bb, (0, 2, 1)).reshape(n_g // S5_SLAB, S5_SLAB, GROUP_P, STATE_N)
        return jnp.einsum("sgpn,gh->sgphn", t, eye).reshape(n_g // S5_SLAB, S5_SLAB * GROUP_P, S5_SLAB * STATE_N).astype(BF16)

    def out_slabs(c):
        t = jnp.transpose(c, (0, 2, 1)).reshape(n_g // S5_SLAB, S5_SLAB, STATE_N, GROUP_P)
        return jnp.einsum("sgnp,gh->sgnhp", t, eye).reshape(n_g // S5_SLAB, S5_SLAB * STATE_N, S5_SLAB * GROUP_P).astype(BF16)

    return dict(w_in=w_in.astype(BF16), bb_re=in_slabs(bb_re), bb_im=in_slabs(bb_im), c_re=out_slabs(c_re),
                c_im=out_slabs(c_im), a_re=ab_re.reshape(1, -1), a_im=ab_im.reshape(1, -1), d=d.reshape(1, -1),
                w_glu=w_glu.astype(BF16), b_glu=b_glu.reshape(1, -1), w_out=w_out.astype(BF16))


def _sgate_body(pt_ref, q_ref, *rest, n_pg, nb):
    pages, idx_ref, km_ref = rest[:n_pg], rest[n_pg], rest[n_pg + 1]
    pc = pl.program_id(1)
    ppb = BLOCK // PAGE_SIZE
    lane = lax.broadcasted_iota(I32, km_ref.shape, 2)

    @pl.when(pc == 0)
    def _():
        km_ref[...] = jnp.zeros(km_ref.shape, F32)

    km = km_ref[...]
    for u in range(n_pg // ppb):
        tot = functools.reduce(jnp.add, [pages[u * ppb + t][0, 0] for t in range(ppb)])
        mean = jnp.sum(tot, axis=-1, keepdims=True) * (1.0 / BLOCK)
        km = jnp.where(lane == pc * (n_pg // ppb) + u, mean, km)
    km_ref[...] = km

    @pl.when(pc == pl.num_programs(1) - 1)
    def _():
        km = km_ref[...]
        bl = lax.broadcasted_iota(I32, (km.shape[0], km.shape[2]), 1)
        for qi in range(q_ref.shape[3]):
            gate = jnp.sum(km * q_ref[0, :, :, qi:qi + 1], axis=1)
            for t, am in enumerate(_top3(jnp.where(bl < nb, gate, -jnp.inf), bl, 1)):
                idx_ref[0, qi, :, t:t + 1] = am


def sample_gate(qt, cache_kt, page_table, li):
    db, nh, dh, nq = qt.shape
    n_pages = page_table.shape[1]
    nb = n_pages * PAGE_SIZE // BLOCK
    assert nb <= LANES
    n_pg = 16
    page = lambda t: pl.BlockSpec((1, 1, nh, dh, PAGE_SIZE),
                                  lambda b, pc, pt: (li, pt[b * n_pages + pc * n_pg + t], 0, 0, 0))
    return pl.pallas_call(
        functools.partial(_sgate_body, n_pg=n_pg, nb=nb),
        grid_spec=pltpu.PrefetchScalarGridSpec(
            num_scalar_prefetch=1, grid=(db, n_pages // n_pg),
            in_specs=[pl.BlockSpec((1, nh, dh, nq), lambda b, pc, pt: (b, 0, 0, 0))] + [page(t) for t in range(n_pg)],
            out_specs=pl.BlockSpec((1, nq, nh, TOPK), lambda b, pc, pt: (b, 0, 0, 0)),
            scratch_shapes=[pltpu.VMEM((nh, dh, LANES), F32)]),
        out_shape=jax.ShapeDtypeStruct((db, nq, nh, TOPK), I32),
        compiler_params=_params("arbitrary", "arbitrary"), name="sample_gate")(
            page_table.reshape(-1), qt, *([cache_kt] * n_pg))


def _sattn_body(pt_ref, idx_ref, q_ref, kn_ref, vn_ref, lo_ref, hi_ref, tab2_ref, ck_hbm, cv_hbm,
                o_ref, kbuf, vbuf, sem, *, nq, nh, n_pages, li, past_len):
    ppb = BLOCK // PAGE_SIZE
    b = pl.program_id(0)
    off = lax.broadcasted_iota(I32, (1, BLOCK), 1)
    head_lane = lax.broadcasted_iota(I32, tab2_ref.shape, 1)
    lo, hi = lo_ref[...], hi_ref[...]
    d_new = lax.broadcasted_iota(I32, (nq, nq), 0) - lax.broadcasted_iota(I32, (nq, nq), 1)

    def page_copies(bb, h, slot, fetch):
        out = []
        for qi in range(nq):
            for t in range(TOPK):
                blk = idx_ref[((bb * nq + qi) * nh + h) * TOPK + t] if fetch else 0
                for pp in range(ppb):
                    page = pt_ref[bb * n_pages + blk * ppb + pp] if fetch else 0
                    j = (qi * TOPK + t) * ppb + pp
                    out.append(pltpu.make_async_copy(ck_hbm.at[li, page, h], kbuf.at[slot, j], sem.at[slot]))
                    out.append(pltpu.make_async_copy(cv_hbm.at[li, page, h], vbuf.at[slot, j], sem.at[slot]))
        return out

    @pl.when(b == 0)
    def _():
        for c in page_copies(0, 0, 0, True):
            c.start()

    def head(h, carry):
        slot = h % 2
        last_head = h == nh - 1
        nb_, nh_ = jnp.where(last_head, b + 1, b), jnp.where(last_head, 0, h + 1)

        @pl.when(nb_ < pl.num_programs(0))
        def _():
            for c in page_copies(nb_, nh_, 1 - slot, True):
                c.start()

        for c in page_copies(0, 0, slot, False):
            c.wait()

        tabcol = jnp.sum(jnp.where(head_lane == h, tab2_ref[...], 0.0), axis=1, keepdims=True)

        def bias_row(dist):
            return jnp.sum(jnp.where((dist >= lo) & (dist < hi), tabcol, 0.0), axis=0, keepdims=True)

        kn, vn = kn_ref[0, h], vn_ref[0, h]
        for qi in range(nq):
            qcol = q_ref[0, h, :, qi:qi + 1] * QK_SCALE
            scores = []
            for t in range(TOPK):
                blk = idx_ref[((b * nq + qi) * nh + h) * TOPK + t]
                bias = bias_row((past_len + qi) - (blk * BLOCK + off))
                for pp in range(ppb):
                    kt = kbuf[slot, (qi * TOPK + t) * ppb + pp]
                    scores.append(jnp.sum(kt * qcol, axis=0, keepdims=True)
                                  + bias[:, pp * PAGE_SIZE:(pp + 1) * PAGE_SIZE])
            s_own = jnp.sum(kn * qcol, axis=0, keepdims=True) + bias_row(d_new[qi:qi + 1, :])
            scores.append(jnp.where(d_new[qi:qi + 1, :] >= 0, s_own, NEG_INF))
            m = functools.reduce(jnp.maximum, [jnp.max(s, axis=1, keepdims=True) for s in scores])
            ps = [jnp.exp(s - m) for s in scores]
            l = functools.reduce(jnp.add, [jnp.sum(p, axis=1, keepdims=True) for p in ps])
            acc = functools.reduce(jnp.add, [vbuf[slot, qi * TOPK * ppb + j] * (ps[j] / l) for j in range(TOPK * ppb)])
            o = jnp.sum(acc, axis=1, keepdims=True) + jnp.sum(vn * (ps[-1] / l), axis=1, keepdims=True)
            o_ref[0, h, :, qi:qi + 1] = o
        return carry

    lax.fori_loop(0, nh, head, 0)


def sample_attn(qt, knt, vnt, cache_kt, cache_vt, page_table, idx, thr, tab, li, past_len):
    db, nh, dh, nq = qt.shape
    assert nh % 2 == 0
    n_pages = page_table.shape[1]
    n_pg = nq * TOPK * (BLOCK // PAGE_SIZE)
    tok = pl.BlockSpec((1, nh, dh, nq), lambda b, pt, ix: (b, 0, 0, 0))
    hbm = pl.BlockSpec(memory_space=pl.ANY)
    whole = lambda a: pl.BlockSpec(a.shape, lambda b, pt, ix: (0,) * a.ndim)
    lo = thr.reshape(-1, 1)
    hi = jnp.concatenate([thr[1:], jnp.full((1,), jnp.iinfo(jnp.int32).max, I32)]).reshape(-1, 1)
    return pl.pallas_call(
        functools.partial(_sattn_body, nq=nq, nh=nh, n_pages=n_pages, li=li, past_len=past_len),
        grid_spec=pltpu.PrefetchScalarGridSpec(
            num_scalar_prefetch=2, grid=(db,),
            in_specs=[tok, tok, tok, whole(lo), whole(hi), whole(tab), hbm, hbm],
            out_specs=tok,
            scratch_shapes=[pltpu.VMEM((2, n_pg, dh, PAGE_SIZE), F32), pltpu.VMEM((2, n_pg, dh, PAGE_SIZE), F32),
                            pltpu.SemaphoreType.DMA((2,))]),
        out_shape=jax.ShapeDtypeStruct((db, nh, dh, nq), F32),
        compiler_params=_params("arbitrary"), name="sample_attn")(
            page_table.reshape(-1), idx.reshape(-1), qt, knt, vnt, lo, hi, tab, cache_kt, cache_vt)


def _bucket_thresholds():
    n = jnp.arange(MAX_DISTANCE + 1, dtype=I32)
    nf = jnp.maximum(n, 1).astype(F32)
    log_b = MAX_EXACT + (jnp.log(nf / MAX_EXACT) / math.log(MAX_DISTANCE / MAX_EXACT)
                         * (N_BUCKETS - MAX_EXACT)).astype(I32)
    bucket = jnp.where(n < MAX_EXACT, n, jnp.minimum(log_b, N_BUCKETS - 1))
    return jnp.sum(bucket[None, :] < jnp.arange(N_BUCKETS, dtype=I32)[:, None], axis=1).astype(I32)


def _heads(t, b, s):
    return t.reshape(b, s, N_HEADS, HEAD_DIM).transpose(0, 2, 1, 3)


def kernel(x_prompt, x_sample, cache_k, cache_v, state_s5_re, state_s5_im, page_table, p_prompt, p_sample, ln_g, ln_b, ffn_w1, ffn_w3, ffn_w2, ple_w_proj, ple_w_gate, rel_bias_table, attn_w_qkv, attn_w_o, s5_w_in, s5_lambda_re, s5_lambda_im, s5_log_dt, s5_b_re, s5_b_im, s5_c_re, s5_c_im, s5_d, s5_w_glu, s5_b_glu, s5_w_out):
    depth = ln_g.shape[0]
    alpha = (2 * depth) ** 0.25
    bsz, seq, d = x_prompt.shape
    db, nq, _ = x_sample.shape
    assert bsz == 1 and seq % (2 * BLOCK) == 0 and d == N_HEADS * HEAD_DIM
    past_len = page_table.shape[1] * PAGE_SIZE
    assert past_len % BLOCK == 0
    xp = x_prompt.reshape(seq, d)
    xs = x_sample.reshape(db * nq, d)
    thr = _bucket_thresholds()
    tab = rel_bias_table.reshape(-1)
    vec = lambda a: a.reshape(1, -1)
    outs = {k: [] for k in ("kp", "vp", "ks", "vs", "spr", "spi", "ssr", "ssi")}
    for i in range(depth):
        li = i // 2
        w1, w3, w2 = ffn_w1[i].astype(BF16), ffn_w3[i].astype(BF16), ffn_w2[i].astype(BF16)
        g, b = ln_g[i], ln_b[i]
        xp = ffn_ln(xp, w1[0], w3[0], w2[0], vec(g[0]), vec(b[0]), alpha)
        xs = ffn_ln(xs, w1[0], w3[0], w2[0], vec(g[0]), vec(b[0]), alpha)
        if i % 2 == 0:
            wqkv = attn_w_qkv[li].astype(BF16)
            wq, wk, wv = wqkv[:, :d], wqkv[:, d:2 * d], wqkv[:, 2 * d:]
            wo = attn_w_o[li].astype(BF16)
            q, kto, vto, kta, vb, km = qkv_prompt(xp, wq, wv, wk.T, wv.T)
            kmt = jnp.transpose(km, (1, 0, 2)).reshape(d, seq // BLOCK)
            o = moba_prompt(q, kta, vb, kmt, thr, tab)
            xp = lin_ln(xp, o, wo, vec(g[1]), vec(b[1]), alpha)
            outs["kp"].append(jnp.swapaxes(kto.reshape(1, N_HEADS, HEAD_DIM, seq), 2, 3))
            outs["vp"].append(jnp.swapaxes(vto.reshape(1, N_HEADS, HEAD_DIM, seq), 2, 3))

            qkv_s = linear(xs, wqkv)
            qs, ks, vs = (_heads(t, db, nq) for t in jnp.split(qkv_s, 3, axis=-1))
            ckt, cvt = jnp.swapaxes(cache_k, 3, 4), jnp.swapaxes(cache_v, 3, 4)
            qt, knt, vnt = (jnp.swapaxes(t, 2, 3) for t in (qs, ks, vs))
            idx = sample_gate(qt, ckt, page_table, li)
            o_s = sample_attn(qt, knt, vnt, ckt, cvt, page_table, idx, thr, rel_bias_table, li, past_len)
            xs = lin_ln(xs, o_s.transpose(0, 3, 1, 2).reshape(db * nq, d), wo, vec(g[1]), vec(b[1]), alpha)
            outs["ks"].append(ks)
            outs["vs"].append(vs)
        else:
            w = s5_weights(s5_w_in[li], s5_lambda_re[li], s5_lambda_im[li], s5_log_dt[li], s5_b_re[li], s5_b_im[li],
                           s5_c_re[li], s5_c_im[li], s5_d[li], s5_w_glu[li], s5_b_glu[li], s5_w_out[li])
            n_state = w["a_re"].shape[1]
            zero = jnp.zeros((1, n_state), F32)
            xp, hr, hi = s5_layer(xp, zero, zero, w, vec(g[1]), vec(b[1]), alpha, nseg=8, lseg=64, cw=1024, chained=True)
            outs["spr"].append(hr.reshape(1, -1, STATE_N))
            outs["spi"].append(hi.reshape(1, -1, STATE_N))
            xs, hr, hi = s5_layer(xs, state_s5_re[li].reshape(db, n_state), state_s5_im[li].reshape(db, n_state), w,
                                  vec(g[1]), vec(b[1]), alpha, nseg=db, lseg=nq, cw=256, chained=False)
            outs["ssr"].append(hr.reshape(db, -1, STATE_N))
            outs["ssi"].append(hi.reshape(db, -1, STATE_N))
        ple_w = (ple_w_gate[i].astype(BF16), ple_w_proj[i].astype(BF16))
        xp = ffn_ln(xp, w1[1], w3[1], w2[1], vec(g[2]), vec(b[2]), alpha, ple=(p_prompt[i].reshape(seq, -1),) + ple_w)
        xs = ffn_ln(xs, w1[1], w3[1], w2[1], vec(g[2]), vec(b[2]), alpha, ple=(p_sample[i].reshape(db * nq, -1),) + ple_w)
    st = lambda k: jnp.stack(outs[k])
    return (xp.reshape(bsz, seq, d), xs.reshape(db, nq, d), st("kp"), st("vp"), st("ks"), st("vs"),
            st("spr"), st("spi"), st("ssr"), st("ssi"))
```
